```python
import math
import jax, jax.numpy as jnp
from jax import lax
import numpy as np

D_MODEL = 1024
BATCH = 16
SEQ = 256
DEPTH = 1
DEC_BATCH = 8
DEC_SEQ = 1024
PAST_LEN = 512

GRID_W = 64
D_HYENA = 512
D_RWKV = 512
RWKV_HEAD = 64
RWKV_HEADS = D_RWKV // RWKV_HEAD
LORA_W = 64
LORA_A = 64
LORA_G = 128
C_IN = 3 * D_HYENA + 3 * D_RWKV + LORA_W + LORA_A + LORA_G
FILT_BANDS = 16
FILT_FEAT = 1 + 2 * FILT_BANDS
FILT_HIDDEN = 64
N_FILT = 2 * D_HYENA
HYENA_TARGET = 1e-2
HYENA_FAST_PCT = 0.3
HYENA_SLOW_PCT = 1.5
D_FF = 2816
ALPHA = (2.0 * DEPTH) ** 0.25
BETA = (8.0 * DEPTH) ** -0.25
LN_EPS = 1e-5
GN_EPS = 64e-5
NORM_EPS = 1e-12

kernel_name = 'hymba_hyena_rwkv7_prefix_dit'


def _layernorm(x, g=None, b=None):
    xf = x.astype(jnp.float32)
    mu = xf.mean(-1, keepdims=True)
    var = jnp.square(xf - mu).mean(-1, keepdims=True)
    y = (xf - mu) * lax.rsqrt(var + LN_EPS)
    if g is not None:
        y = y * g.astype(jnp.float32) + b.astype(jnp.float32)
    return y.astype(x.dtype)


def _ada(cond, ada_w, ada_b):
    m = jax.nn.silu(cond) @ ada_w + ada_b
    return jnp.split(m[..., None, :], 6, axis=-1)


def _dwconv1d(x, w):
    return lax.conv_general_dilated(x, w[:, None, :].astype(x.dtype), (1,), ((1, 1),),
                                    dimension_numbers=('NWC', 'WIO', 'NWC'),
                                    feature_group_count=x.shape[-1])


def _dwconv2d_grid(x, w):
    B, L, C = x.shape
    rows = L // GRID_W
    xg = x.reshape(B, rows, GRID_W, C)
    y = lax.conv_general_dilated(xg, w[:, :, None, :].astype(x.dtype), (1, 1), ((1, 1), (1, 1)),
                                 dimension_numbers=('NHWC', 'HWIO', 'NHWC'),
                                 feature_group_count=C)
    return y.reshape(B, L, C)


def _hyena_filters(L, filt_w1, filt_b1, filt_w2, filt_b2, filt_w3, filt_freq):
    f32 = jnp.float32
    t = jnp.arange(L, dtype=f32)[:, None] / L
    bands = jnp.arange(1, FILT_BANDS + 1, dtype=f32)[None, :]
    ang = (2.0 * math.pi) * bands * t
    feat = jnp.concatenate([t, jnp.sin(ang), jnp.cos(ang)], axis=-1)
    h = jnp.sin(filt_freq[0].astype(f32) * (feat @ filt_w1.astype(f32) + filt_b1.astype(f32)))
    h = jnp.sin(filt_freq[1].astype(f32) * (h @ filt_w2.astype(f32) + filt_b2.astype(f32)))
    h = h @ filt_w3.astype(f32)
    slow = abs(math.log(HYENA_TARGET) / HYENA_SLOW_PCT)
    fast = abs(math.log(HYENA_TARGET) / HYENA_FAST_PCT)
    deltas = jnp.linspace(slow, fast, N_FILT, dtype=f32)
    h = h * jnp.exp(-t * deltas[None, :])
    h = h.reshape(L, 2, D_HYENA)
    return h / jnp.sum(jnp.abs(h), axis=(0, 1), keepdims=True)


def _fftconv(z, h_fwd, h_bwd, bias):
    L = z.shape[1]
    h_full = jnp.concatenate([h_fwd, jnp.zeros_like(h_fwd[:1]), h_bwd[:0:-1]], axis=0)
    Hf = jnp.fft.rfft(h_full, n=2 * L, axis=0)
    Zf = jnp.fft.rfft(z, n=2 * L, axis=1)
    y = jnp.fft.irfft(Zf * Hf[None], n=2 * L, axis=1)[:, :L]
    return y + z * bias


def _rwkv_scan(r, w, k, v, a_vec, b_vec, s0, reverse):
    def step(S, inp):
        r_t, w_t, k_t, v_t, a_t, b_t = inp
        sa = jnp.einsum('bhvk,bhk->bhv', S, a_t)
        S = S * w_t[:, :, None, :] + sa[..., None] * b_t[:, :, None, :] + v_t[..., None] * k_t[:, :, None, :]
        return S, jnp.einsum('bhvk,bhk->bhv', S, r_t)
    xs = tuple(jnp.moveaxis(t, 1, 0) for t in (r, w, k, v, a_vec, b_vec))
    S, ys = lax.scan(step, s0, xs, reverse=reverse)
    return jnp.moveaxis(ys, 0, 1), S


def _rwkv7(r, k, v, wd, ad, gd, s0, p):
    f32 = jnp.float32
    B, L, _ = r.shape
    heads = lambda t: t.reshape(B, L, RWKV_HEADS, RWKV_HEAD)
    r, k, v, wd, ad, gd = (t.astype(f32) for t in (r, k, v, wd, ad, gd))
    g = jax.nn.sigmoid(gd) @ p['rwkv_g_lora'].astype(f32)
    kk = heads(k * p['rwkv_k_k'].astype(f32))
    kk = kk / jnp.maximum(jnp.sqrt(jnp.sum(kk * kk, -1, keepdims=True)), NORM_EPS)
    rh, vh = heads(r), heads(v)
    y = jnp.zeros_like(rh)
    bonus = jnp.zeros_like(rh)
    states = []
    for d in range(2):
        w_raw = p['rwkv_w0'][d].astype(f32) + jnp.tanh(wd) @ p['rwkv_w_lora'][d].astype(f32)
        decay = jnp.exp(-jnp.exp(-jax.nn.softplus(-w_raw) - 0.5))
        a = jax.nn.sigmoid(p['rwkv_a0'][d].astype(f32) + ad @ p['rwkv_a_lora'][d].astype(f32))
        k_d = heads(k * (1.0 + (a - 1.0) * p['rwkv_k_a'].astype(f32)))
        y_d, S_d = _rwkv_scan(rh, heads(decay), k_d, vh, -kk, kk * heads(a),
                              s0[:, d].astype(f32), reverse=(d == 1))
        y = y + y_d
        bonus = bonus + jnp.sum(rh * k_d * p['rwkv_r_k'].astype(f32), -1, keepdims=True) * vh
        states.append(S_d)
    mu = y.mean(-1, keepdims=True)
    var = jnp.square(y - mu).mean(-1, keepdims=True)
    y = ((y - mu) * lax.rsqrt(var + GN_EPS)).reshape(B, L, D_RWKV)
    y = y * p['lnx_g'].astype(f32) + p['lnx_b'].astype(f32) + bonus.reshape(B, L, D_RWKV)
    return y * g, jnp.stack(states, axis=1)


def _layer(x, cond, s0, on_grid, p):
    dt = x.dtype
    L = x.shape[1]
    sh1, sc1, g1, sh2, sc2, g2 = _ada(cond, p['ada_w'], p['ada_b'])
    h = _layernorm(x) * (1.0 + sc1) + sh1
    P = _dwconv1d(h @ p['w_in'], p['conv_in'])
    splits = [int(s) for s in np.cumsum([D_HYENA] * 3 + [D_RWKV] * 3 + [LORA_W, LORA_A])]
    hv, hx0, hx1, r, k, v, wd, ad, gd = jnp.split(P, splits, axis=-1)
    filt = _hyena_filters(L, p['filt_w1'], p['filt_b1'], p['filt_w2'], p['filt_b2'],
                          p['filt_w3'], p['filt_freq'])
    f32 = jnp.float32
    y_h = hx0.astype(f32) * _fftconv(hx1.astype(f32) * hv.astype(f32), filt[:, 0], filt[:, 1],
                                      p['hyena_bias'].astype(f32))
    y_r, s_final = _rwkv7(r, k, v, wd, ad, gd, s0, p)
    mix = jnp.concatenate([y_h, y_r], axis=-1).astype(dt) @ p['w_out']
    x = _layernorm(ALPHA * x + g1 * mix, p['ln1_g'], p['ln1_b'])
    h = _layernorm(x) * (1.0 + sc2) + sh2
    u, gt = jnp.split(h @ p['w_ffn_up'], 2, axis=-1)
    if on_grid:
        u = _dwconv2d_grid(u, p['ffn_conv_w'])
    else:
        u = _dwconv1d(u, p['ffn_conv_w'][1])
    u = u + p['ffn_conv_b']
    ff = (jax.nn.gelu(u) * gt) @ p['w_ffn_down']
    x = _layernorm(ALPHA * x + g2 * ff, p['ln2_g'], p['ln2_b'])
    return x, s_final


def setup_inputs(seed: int = 0) -> dict:
    key = jax.random.key(seed)
    ks = iter(jax.random.split(key, 48))
    f32 = jnp.float32
    nrm = lambda shape, scale: jax.random.normal(next(ks), shape, f32) * scale
    Ld = DEPTH
    conv3_base = jnp.zeros((3,), f32).at[1].set(0.5).at[0].set(0.25).at[2].set(0.25)
    conv33_base = jnp.zeros((3, 3), f32).at[1, 1].set(1.0)
    return {
        'x_prompt': nrm((BATCH, SEQ, D_MODEL), 1.0),
        'x_sample': nrm((DEC_BATCH, DEC_SEQ, D_MODEL), 1.0),
        'state_rwkv': nrm((DEC_BATCH, DEPTH, 2, RWKV_HEADS, RWKV_HEAD, RWKV_HEAD), 0.5),
        'c': nrm((DEC_BATCH, D_MODEL), 1.0),
        'c_ctx': nrm((D_MODEL,), 1.0),
        'ada_w': nrm((Ld, D_MODEL, 6 * D_MODEL), 0.5 * D_MODEL ** -0.5),
        'ada_b': nrm((Ld, 6 * D_MODEL), 0.01),
        'w_in': nrm((Ld, D_MODEL, C_IN), D_MODEL ** -0.5),
        'conv_in': conv3_base[None, :, None] + nrm((Ld, 3, C_IN), 0.1),
        'filt_w1': nrm((Ld, FILT_FEAT, FILT_HIDDEN), FILT_FEAT ** -0.5),
        'filt_b1': nrm((Ld, FILT_HIDDEN), 0.1),
        'filt_w2': nrm((Ld, FILT_HIDDEN, FILT_HIDDEN), FILT_HIDDEN ** -0.5),
        'filt_b2': nrm((Ld, FILT_HIDDEN), 0.1),
        'filt_w3': nrm((Ld, FILT_HIDDEN, N_FILT), FILT_HIDDEN ** -0.5),
        'filt_freq': 1.0 + nrm((Ld, 2, FILT_HIDDEN), 0.1),
        'hyena_bias': nrm((Ld, D_HYENA), 0.1),
        'rwkv_w0': jax.random.uniform(next(ks), (Ld, 2, D_RWKV), f32, -5.0, 1.0),
        'rwkv_w_lora': nrm((Ld, 2, LORA_W, D_RWKV), LORA_W ** -0.5),
        'rwkv_a0': nrm((Ld, 2, D_RWKV), 0.5),
        'rwkv_a_lora': nrm((Ld, 2, LORA_A, D_RWKV), LORA_A ** -0.5),
        'rwkv_g_lora': nrm((Ld, LORA_G, D_RWKV), LORA_G ** -0.5),
        'rwkv_k_k': 0.85 + nrm((Ld, D_RWKV), 0.05),
        'rwkv_k_a': 1.0 + nrm((Ld, D_RWKV), 0.05),
        'rwkv_r_k': nrm((Ld, RWKV_HEADS, RWKV_HEAD), 0.1),
        'lnx_g': 1.0 + nrm((Ld, D_RWKV), 0.05),
        'lnx_b': nrm((Ld, D_RWKV), 0.01),
        'w_out': nrm((Ld, D_MODEL, D_MODEL), BETA * D_MODEL ** -0.5),
        'ln1_g': 1.0 + nrm((Ld, D_MODEL), 0.05),
        'ln1_b': nrm((Ld, D_MODEL), 0.01),
        'w_ffn_up': nrm((Ld, D_MODEL, 2 * D_FF), D_MODEL ** -0.5),
        'ffn_conv_w': conv33_base[None, :, :, None] + nrm((Ld, 3, 3, D_FF), 0.1),
        'ffn_conv_b': nrm((Ld, D_FF), 0.01),
        'w_ffn_down': nrm((Ld, D_FF, D_MODEL), BETA * D_FF ** -0.5),
        'ln2_g': 1.0 + nrm((Ld, D_MODEL), 0.05),
        'ln2_b': nrm((Ld, D_MODEL), 0.01),
    }


def reference(x_prompt, x_sample, state_rwkv, c, c_ctx, ada_w, ada_b, w_in, conv_in,
              filt_w1, filt_b1, filt_w2, filt_b2, filt_w3, filt_freq, hyena_bias,
              rwkv_w0, rwkv_w_lora, rwkv_a0, rwkv_a_lora, rwkv_g_lora, rwkv_k_k, rwkv_k_a,
              rwkv_r_k, lnx_g, lnx_b, w_out, ln1_g, ln1_b, w_ffn_up, ffn_conv_w, ffn_conv_b,
              w_ffn_down, ln2_g, ln2_b):
    stacked = dict(ada_w=ada_w, ada_b=ada_b, w_in=w_in, conv_in=conv_in,
                   filt_w1=filt_w1, filt_b1=filt_b1, filt_w2=filt_w2, filt_b2=filt_b2,
                   filt_w3=filt_w3, filt_freq=filt_freq, hyena_bias=hyena_bias,
                   rwkv_w0=rwkv_w0, rwkv_w_lora=rwkv_w_lora, rwkv_a0=rwkv_a0,
                   rwkv_a_lora=rwkv_a_lora, rwkv_g_lora=rwkv_g_lora, rwkv_k_k=rwkv_k_k,
                   rwkv_k_a=rwkv_k_a, rwkv_r_k=rwkv_r_k, lnx_g=lnx_g, lnx_b=lnx_b,
                   w_out=w_out, ln1_g=ln1_g, ln1_b=ln1_b, w_ffn_up=w_ffn_up,
                   ffn_conv_w=ffn_conv_w, ffn_conv_b=ffn_conv_b, w_ffn_down=w_ffn_down,
                   ln2_g=ln2_g, ln2_b=ln2_b)
    b_ctx = x_prompt.shape[0]
    zero_state = jnp.zeros((b_ctx, 2, RWKV_HEADS, RWKV_HEAD, RWKV_HEAD), jnp.float32)
    y_prompt = x_prompt
    y_sample = x_sample
    ctx_states = []
    for layer in range(DEPTH):
        p = {name: arr[layer] for name, arr in stacked.items()}
        y_prompt, s_ctx = _layer(y_prompt, c_ctx, zero_state, False, p)
        ctx_states.append(s_ctx)
        y_sample, _ = _layer(y_sample, c, state_rwkv[:, layer], True, p)
    new_state_rwkv = jnp.stack(ctx_states, axis=1)
    return (y_prompt, y_sample, new_state_rwkv)
```

```python
import functools
import math

import ml_dtypes
import numpy as np
import jax
import jax.numpy as jnp
from jax import lax
from jax.experimental import pallas as pl
from jax.experimental.pallas import tpu as pltpu

F32 = jnp.float32
BF16 = jnp.bfloat16

D_MODEL = 1024
D_HYENA = 512
D_RWKV = 512
HEAD = 64
HEADS = D_RWKV // HEAD
LORA_W = 64
LORA_A = 64
LORA_G = 128
LORA_ALL = LORA_W + LORA_A + LORA_G
C_IN = 3 * D_HYENA + 3 * D_RWKV + LORA_ALL
FILT_BANDS = 16
FILT_FEAT = 1 + 2 * FILT_BANDS
FILT_HIDDEN = 64
N_FILT = 2 * D_HYENA
HYENA_TARGET = 1e-2
HYENA_FAST_PCT = 0.3
HYENA_SLOW_PCT = 1.5
D_FF = 2816
GRID_W = 64
DEPTH = 1
ALPHA = (2.0 * DEPTH) ** 0.25
LN_EPS = 1e-5
GN_EPS = 64e-5
NORM_EPS = 1e-12

LANES = 128
SUBLANES = 8
MXU_N = 256
ROW_TILE = 1024
SCAN_STEPS = 16
VMEM_LIMIT = 56 * 1024 * 1024


def _cparams(sem):
    return pltpu.CompilerParams(dimension_semantics=sem, vmem_limit_bytes=VMEM_LIMIT)


def _dot(a, b):
    return jnp.dot(a, b, preferred_element_type=F32)


def _split(x):
    hi = x.astype(BF16)
    lo = (x - hi.astype(F32)).astype(BF16)
    return hi, lo


def _dot_hi_const(x, c):
    hi, lo = _split(x)
    return _dot(hi, c) + _dot(lo, c)


def _dot3(ah, al, bh, bl):
    return _dot(ah, bh) + _dot(ah, bl) + _dot(al, bh)


def _layernorm_rows(x):
    mu = jnp.mean(x, axis=-1, keepdims=True)
    xc = x - mu
    var = jnp.mean(xc * xc, axis=-1, keepdims=True)
    return xc * lax.rsqrt(var + LN_EPS)


def _row_tile(tokens, seq):
    for tm in (ROW_TILE, 512, 256):
        if tokens % tm == 0 and (tm % seq == 0 or seq % tm == 0):
            return tm
    raise ValueError(f"no row tile for {tokens} tokens of sequence length {seq}")


def _np_split(x64):
    hi = x64.astype(ml_dtypes.bfloat16)
    lo = (x64 - hi.astype(np.float64)).astype(ml_dtypes.bfloat16)
    return hi, lo


@functools.lru_cache(maxsize=None)
def _dft_constants(L):
    N = 2 * L
    f = np.arange(L, dtype=np.int64)[:, None]
    t = np.arange(L, dtype=np.int64)[None, :]
    ang = (2.0 * np.pi / N) * ((f * t) % N).astype(np.float64)
    cos, sin = np.cos(ang), np.sin(ang)
    nyq = np.where(np.arange(L) % 2 == 0, 1.0, -1.0)
    fwd_s = sin.copy()
    fwd_s[0, :] = nyq
    fwd = np.concatenate([cos, fwd_s], axis=0)
    wf = np.full((L, 1), 2.0)
    wf[0, 0] = 1.0
    inv_c = (wf * cos).T / N
    inv_s = (2.0 * sin).T / N
    inv_s[:, 0] = nyq / N
    inv = np.concatenate([inv_c, inv_s], axis=1)
    return _np_split(fwd) + _np_split(inv)


@functools.lru_cache(maxsize=None)
def _filter_constants(L):
    t = np.arange(L, dtype=np.float64)[:, None] / L
    bands = np.arange(1, FILT_BANDS + 1, dtype=np.float64)[None, :]
    ang = (2.0 * math.pi) * bands * t
    feat = np.concatenate([t, np.sin(ang), np.cos(ang)], axis=-1)
    feat = np.pad(feat, ((0, 0), (0, FILT_HIDDEN - FILT_FEAT)))
    slow = abs(math.log(HYENA_TARGET) / HYENA_SLOW_PCT)
    fast = abs(math.log(HYENA_TARGET) / HYENA_FAST_PCT)
    deltas = np.linspace(slow, fast, N_FILT, dtype=np.float64)
    window = np.exp(-t * deltas[None, :])
    return feat.astype(np.float32), window.astype(np.float32)


@functools.lru_cache(maxsize=None)
def _head_ones():
    h = np.arange(D_RWKV) // HEAD
    return (h[:, None] == h[None, :]).astype(ml_dtypes.bfloat16)


def _ada_kernel(c_ref, w_ref, b_ref, o_ref):
    c = c_ref[...]
    s = c * jax.nn.sigmoid(c)
    o_ref[...] = _dot(s.astype(BF16), w_ref[...].astype(BF16)) + b_ref[...]


def _ada(cond, ada_w, ada_b):
    rows = cond.shape[0]
    tn = 1024
    return pl.pallas_call(
        _ada_kernel,
        grid=(6 * D_MODEL // tn,),
        in_specs=[pl.BlockSpec((rows, D_MODEL), lambda j: (0, 0)),
                  pl.BlockSpec((D_MODEL, tn), lambda j: (0, j)),
                  pl.BlockSpec((1, tn), lambda j: (0, j))],
        out_specs=pl.BlockSpec((rows, tn), lambda j: (0, j)),
        out_shape=jax.ShapeDtypeStruct((rows, 6 * D_MODEL), F32),
        compiler_params=_cparams(("arbitrary",)),
    )(cond, ada_w, ada_b)


def _inproj_kernel(x_ref, mod_ref, w_ref, cw_ref, o_ref, h_ref, *, seq):
    @pl.when(pl.program_id(1) == 0)
    def _():
        m = mod_ref[0]
        h = _layernorm_rows(x_ref[...]) * (1.0 + m[1:2]) + m[0:1]
        h_ref[...] = h.astype(BF16)

    p = _dot(h_ref[...], w_ref[...])
    rows = p.shape[0]
    t = lax.broadcasted_iota(jnp.int32, p.shape, 0) & (seq - 1)
    prev = jnp.where(t == 0, 0.0, pltpu.roll(p, 1, 0))
    nxt = jnp.where(t == seq - 1, 0.0, pltpu.roll(p, rows - 1, 0))
    cw = cw_ref[...]
    o_ref[...] = cw[0:1] * prev + cw[1:2] * p + cw[2:3] * nxt


def _inproj(x2, mod, w_in_bf, conv_in, seq):
    tokens = x2.shape[0]
    tm = _row_tile(tokens, seq)
    nmod = mod.shape[0]
    assert nmod == 1 or tm <= seq
    tn = MXU_N
    return pl.pallas_call(
        functools.partial(_inproj_kernel, seq=seq),
        grid=(tokens // tm, C_IN // tn),
        in_specs=[pl.BlockSpec((tm, D_MODEL), lambda i, j: (i, 0)),
                  pl.BlockSpec((1, 6, D_MODEL), lambda i, j: ((i * tm // seq) % nmod, 0, 0)),
                  pl.BlockSpec((D_MODEL, tn), lambda i, j: (0, j)),
                  pl.BlockSpec((3, tn), lambda i, j: (0, j))],
        out_specs=pl.BlockSpec((tm, tn), lambda i, j: (i, j)),
        out_shape=jax.ShapeDtypeStruct((tokens, C_IN), F32),
        scratch_shapes=[pltpu.VMEM((tm, D_MODEL), BF16)],
        compiler_params=_cparams(("arbitrary", "arbitrary")),
    )(x2, mod, w_in_bf, conv_in)


def _filter_kernel(feat_ref, win_ref, w1_ref, b1_ref, w2_ref, b2_ref, w3_ref, fr_ref,
                   fh_ref, fl_ref, a_ref, an_ref, b_ref, *, seq):
    def hp_dot(x, w):
        xh, xl = _split(x)
        wh, wl = _split(w)
        return _dot3(xh, xl, wh, wl)

    fr = fr_ref[...]
    h = jnp.sin(fr[0:1] * (hp_dot(feat_ref[...], w1_ref[...]) + b1_ref[...]))
    h = jnp.sin(fr[1:2] * (hp_dot(h, w2_ref[...]) + b2_ref[...]))
    h = hp_dot(h, w3_ref[...]) * win_ref[...]
    col = jnp.sum(jnp.abs(h), axis=0, keepdims=True)
    inv = 1.0 / (col[:, :D_HYENA] + col[:, D_HYENA:])
    hf = h[:, :D_HYENA] * inv
    hb = h[:, D_HYENA:] * inv
    row = lax.broadcasted_iota(jnp.int32, hf.shape, 0)
    hb = jnp.where(row == 0, 0.0, hb)
    eh, el = _split(hf + hb)
    oh, ol = _split(hf - hb)
    fh, fl = fh_ref[...], fl_ref[...]
    fe = _dot3(fh, fl, eh, el)
    fo = _dot3(fh[seq:], fl[seq:], oh, ol)
    a = fe[:seq]
    a_ref[...] = a
    an_ref[...] = jnp.where(row == 0, fe[seq:seq + 1], a)
    b_ref[...] = jnp.where(row == 0, 0.0, fo)


def _hyena_filter_spectrum(seq, w1, b1, w2, b2, w3, freq):
    feat, window = _filter_constants(seq)
    fh, fl, _, _ = _dft_constants(seq)
    w1p = jnp.pad(w1, ((0, FILT_HIDDEN - FILT_FEAT), (0, 0)))
    shp = jax.ShapeDtypeStruct((seq, D_HYENA), F32)
    return pl.pallas_call(
        functools.partial(_filter_kernel, seq=seq),
        out_shape=(shp, shp, shp),
        compiler_params=pltpu.CompilerParams(vmem_limit_bytes=VMEM_LIMIT),
    )(feat, window, w1p, b1[None], w2, b2[None], w3, freq, fh, fl)


def _hyena_kernel(hv_ref, hx0_ref, hx1_ref, fh_ref, fl_ref, gh_ref, gl_ref, a_ref, an_ref, b_ref,
                  bias_ref, o_ref, *, seq):
    z = hx1_ref[...] * hv_ref[...]
    zh, zl = _split(z)
    zf = _dot3(fh_ref[...], fl_ref[...], zh, zl)
    zc, zs = zf[:seq], zf[seq:]
    a, an, b = a_ref[...], an_ref[...], b_ref[...]
    yc = zc * a - zs * b
    ys = zc * b + zs * an
    yh, yl = _split(jnp.concatenate([yc, ys], axis=0))
    y = _dot3(gh_ref[...], gl_ref[...], yh, yl)
    o_ref[...] = (hx0_ref[...] * (y + bias_ref[...] * z)).astype(o_ref.dtype)


def _hyena(p3, spec, bias, seq):
    batch = p3.shape[0]
    fh, fl, gh, gl = _dft_constants(seq)
    a, an, b = spec
    cb = MXU_N
    ncb = D_HYENA // cb
    const = lambda shape: pl.BlockSpec(shape, lambda c, i: (0, 0), pipeline_mode=pl.Buffered(1))
    col = lambda k: pl.BlockSpec((None, seq, cb), lambda c, i: (i, 0, k * ncb + c))
    chan = lambda rows: pl.BlockSpec((rows, cb), lambda c, i: (0, c))
    return pl.pallas_call(
        functools.partial(_hyena_kernel, seq=seq),
        grid=(ncb, batch),
        in_specs=[col(0), col(1), col(2),
                  const((2 * seq, seq)), const((2 * seq, seq)), const((seq, 2 * seq)), const((seq, 2 * seq)),
                  chan(seq), chan(seq), chan(seq), chan(1)],
        out_specs=pl.BlockSpec((None, seq, cb), lambda c, i: (i, 0, c)),
        out_shape=jax.ShapeDtypeStruct((batch, seq, D_HYENA), BF16),
        compiler_params=_cparams(("arbitrary", "arbitrary")),
    )(p3, p3, p3, fh, fl, gh, gl, a, an, b, bias)


def _rwkv_prep_kernel(r_ref, k_ref, v_ref, lora_ref, ones_ref, w0_ref, wl_ref, a0_ref, al_ref, gl_ref,
                      kk_ref, ka_ref, rk_ref,
                      dec0_ref, dec1_ref, kd0_ref, kd1_ref, b0_ref, b1_ref, nkk_ref, bonus_ref, g_ref):
    r, k, v = r_ref[...], k_ref[...], v_ref[...]
    lora = lora_ref[...]
    wd = lora[:, :LORA_W]
    ad = lora[:, LORA_W:LORA_W + LORA_A]
    gd = lora[:, LORA_W + LORA_A:]
    ones = ones_ref[...]
    g_ref[...] = _dot(jax.nn.sigmoid(gd).astype(BF16), gl_ref[...].astype(BF16))
    kk = k * kk_ref[...]
    norm = jnp.sqrt(_dot_hi_const(kk * kk, ones))
    kk = kk / jnp.maximum(norm, NORM_EPS)
    nkk_ref[...] = -kk
    tw = jnp.tanh(wd).astype(BF16)
    adb = ad.astype(BF16)
    ksum = jnp.zeros_like(k)
    for d, (dec_ref, kd_ref, b_ref) in enumerate(((dec0_ref, kd0_ref, b0_ref), (dec1_ref, kd1_ref, b1_ref))):
        w_raw = w0_ref[d:d + 1] + _dot(tw, wl_ref[d].astype(BF16))
        dec_ref[...] = jnp.exp(-math.exp(-0.5) * jax.nn.sigmoid(w_raw))
        a = jax.nn.sigmoid(a0_ref[d:d + 1] + _dot(adb, al_ref[d].astype(BF16)))
        kd = k * (1.0 + (a - 1.0) * ka_ref[...])
        kd_ref[...] = kd
        b_ref[...] = kk * a
        ksum = ksum + kd
    bonus_ref[...] = _dot_hi_const(r * ksum * rk_ref[...], ones) * v


def _rwkv_prep(p2, seq, prm):
    tokens = p2.shape[0]
    tm = 512 if tokens % 512 == 0 else 256
    cbase = 3 * D_HYENA // D_RWKV
    col = lambda k: pl.BlockSpec((tm, D_RWKV), lambda i: (i, cbase + k))
    full = lambda a: pl.BlockSpec(a.shape, lambda i: (0,) * a.ndim)
    params = [prm['ones'], prm['rwkv_w0'], prm['rwkv_w_lora'], prm['rwkv_a0'], prm['rwkv_a_lora'],
              prm['rwkv_g_lora'], prm['rwkv_k_k'], prm['rwkv_k_a'], prm['rwkv_r_k']]
    out = jax.ShapeDtypeStruct((tokens, D_RWKV), F32)
    return pl.pallas_call(
        _rwkv_prep_kernel,
        grid=(tokens // tm,),
        in_specs=[col(0), col(1), col(2),
                  pl.BlockSpec((tm, LORA_ALL), lambda i: (i, (C_IN - LORA_ALL) // LORA_ALL))]
                 + [full(a) for a in params],
        out_specs=[pl.BlockSpec((tm, D_RWKV), lambda i: (i, 0))] * 9,
        out_shape=(out,) * 9,
        compiler_params=_cparams(("arbitrary",)),
    )(p2, p2, p2, p2, *params)


def _scan_kernel(r_ref, w_ref, k_ref, v_ref, a_ref, b_ref, s0_ref, y_ref, st_ref, s_ref):
    tc = pl.program_id(1)

    @pl.when(tc == 0)
    def _():
        s_ref[...] = s0_ref[...]

    def step(s, carry):
        a, w, kk, b, r = a_ref[s], w_ref[s], k_ref[s], b_ref[s], r_ref[s]

        def vblock(vb, c2):
            base = pl.multiple_of(vb * SUBLANES, SUBLANES)
            v8 = v_ref[s, pl.ds(base, SUBLANES), :]
            ys = []
            for i in range(SUBLANES):
                sv = s_ref[base + i]
                sa = jnp.sum(sv * a, axis=0, keepdims=True)
                sn = sv * w + sa * b + v8[i:i + 1, :] * kk
                s_ref[base + i] = sn
                ys.append(jnp.sum(sn * r, axis=0, keepdims=True))
            y_ref[s, pl.ds(base, SUBLANES), :] = jnp.concatenate(ys, axis=0)
            return c2

        lax.fori_loop(0, HEAD // SUBLANES, vblock, 0)
        return carry

    lax.fori_loop(0, SCAN_STEPS, step, 0)

    @pl.when(tc == pl.num_programs(1) - 1)
    def _():
        st_ref[...] = s_ref[...]


def _scan(r, w, k, v, a, b, s0):
    seq, _, lanes = r.shape
    tt = SCAN_STEPS
    xin = pl.BlockSpec((tt, HEAD, LANES), lambda l, t: (t, 0, l))
    st = pl.BlockSpec((HEAD, HEAD, LANES), lambda l, t: (0, 0, l))
    return pl.pallas_call(
        _scan_kernel,
        grid=(lanes // LANES, seq // tt),
        in_specs=[xin] * 6 + [st],
        out_specs=[xin, st],
        out_shape=(jax.ShapeDtypeStruct((seq, HEAD, lanes), F32),
                   jax.ShapeDtypeStruct((HEAD, HEAD, lanes), F32)),
        scratch_shapes=[pltpu.VMEM((HEAD, HEAD, LANES), F32)],
        compiler_params=_cparams(("arbitrary", "arbitrary")),
    )(r, w, k, v, a, b, s0)


def _to_lanes(x2, batch, seq, reverse):
    x = jnp.transpose(x2.reshape(batch, seq, HEADS, HEAD), (1, 3, 0, 2)).reshape(seq, HEAD, batch * HEADS)
    return x[::-1] if reverse else x


def _both(xf, xb, batch, seq):
    return jnp.concatenate([_to_lanes(xf, batch, seq, False), _to_lanes(xb, batch, seq, True)], axis=-1)


def _mix_kernel(y_ref, bonus_ref, g_ref, yh_ref, x_ref, mod_ref, ones_ref, wo_ref, lng_ref, lnb_ref,
                l1g_ref, l1b_ref, o_ref):
    ones = ones_ref[...]
    y = y_ref[...]
    mu = _dot_hi_const(y, ones) * (1.0 / HEAD)
    yc = y - mu
    var = _dot_hi_const(yc * yc, ones) * (1.0 / HEAD)
    yr = yc * lax.rsqrt(var + GN_EPS) * lng_ref[...] + lnb_ref[...] + bonus_ref[...]
    yr = (yr * g_ref[...]).astype(BF16)
    mix = _dot(yh_ref[...], wo_ref[:D_HYENA, :]) + _dot(yr, wo_ref[D_HYENA:, :])
    m = mod_ref[0]
    z = _layernorm_rows(ALPHA * x_ref[...] + m[2:3] * mix)
    o_ref[...] = z * l1g_ref[...] + l1b_ref[...]


def _mix(y2, bonus, g, yh2, x2, mod, seq, prm):
    tokens = x2.shape[0]
    tm = 512 if tokens % 512 == 0 else 256
    nmod = mod.shape[0]
    assert nmod == 1 or tm <= seq
    row = lambda width: pl.BlockSpec((tm, width), lambda i: (i, 0))
    full = lambda a: pl.BlockSpec(a.shape, lambda i: (0,) * a.ndim)
    params = [prm['ones'], prm['w_out'], prm['lnx_g'], prm['lnx_b'], prm['ln1_g'], prm['ln1_b']]
    return pl.pallas_call(
        _mix_kernel,
        grid=(tokens // tm,),
        in_specs=[row(D_RWKV), row(D_RWKV), row(D_RWKV), row(D_HYENA), row(D_MODEL),
                  pl.BlockSpec((1, 6, D_MODEL), lambda i: ((i * tm // seq) % nmod, 0, 0))]
                 + [full(a) for a in params],
        out_specs=row(D_MODEL),
        out_shape=jax.ShapeDtypeStruct((tokens, D_MODEL), F32),
        compiler_params=_cparams(("arbitrary",)),
    )(y2, bonus, g, yh2, x2, mod, *params)


def _ffn_kernel(x_ref, mod_ref, wu_ref, wg_ref, cw_ref, cb_ref, wd_ref, l2g_ref, l2b_ref, o_ref,
                h_ref, acc_ref, *, seq, on_grid):
    j = pl.program_id(1)

    @pl.when(j == 0)
    def _():
        m = mod_ref[0]
        h = _layernorm_rows(x_ref[...]) * (1.0 + m[4:5]) + m[3:4]
        h_ref[...] = h.astype(BF16)
        acc_ref[...] = jnp.zeros_like(acc_ref)

    h = h_ref[...]
    u = _dot(h, wu_ref[...])
    gt = _dot(h, wg_ref[...])
    rows = u.shape[0]
    t = lax.broadcasted_iota(jnp.int32, u.shape, 0)
    cw = cw_ref[...]
    if on_grid:
        col = t & (GRID_W - 1)
        grow = (t & (seq - 1)) // GRID_W
        um = jnp.where(col == 0, 0.0, pltpu.roll(u, 1, 0))
        up = jnp.where(col == GRID_W - 1, 0.0, pltpu.roll(u, rows - 1, 0))
        hrow = [cw[3 * di:3 * di + 1] * um + cw[3 * di + 1:3 * di + 2] * u + cw[3 * di + 2:3 * di + 3] * up
                for di in range(3)]
        u = (hrow[1]
             + jnp.where(grow == 0, 0.0, pltpu.roll(hrow[0], GRID_W, 0))
             + jnp.where(grow == seq // GRID_W - 1, 0.0, pltpu.roll(hrow[2], rows - GRID_W, 0)))
    else:
        ts = t & (seq - 1)
        um = jnp.where(ts == 0, 0.0, pltpu.roll(u, 1, 0))
        up = jnp.where(ts == seq - 1, 0.0, pltpu.roll(u, rows - 1, 0))
        u = cw[3:4] * um + cw[4:5] * u + cw[5:6] * up
    u = u + cb_ref[...]
    act = 0.5 * u * (1.0 + jnp.tanh(math.sqrt(2.0 / math.pi) * (u + 0.044715 * (u * u * u))))
    acc_ref[...] += _dot((act * gt).astype(BF16), wd_ref[...])

    @pl.when(j == pl.num_programs(1) - 1)
    def _():
        m = mod_ref[0]
        z = _layernorm_rows(ALPHA * x_ref[...] + m[5:6] * acc_ref[...])
        o_ref[...] = z * l2g_ref[...] + l2b_ref[...]


def _ffn(x2, mod, seq, on_grid, prm):
    tokens = x2.shape[0]
    tm = _row_tile(tokens, seq)
    nmod = mod.shape[0]
    assert nmod == 1 or tm <= seq
    assert not on_grid or tm % seq == 0
    tf = MXU_N
    nf = D_FF // tf
    full = lambda a: pl.BlockSpec(a.shape, lambda i, j: (0,) * a.ndim)
    return pl.pallas_call(
        functools.partial(_ffn_kernel, seq=seq, on_grid=on_grid),
        grid=(tokens // tm, nf),
        in_specs=[pl.BlockSpec((tm, D_MODEL), lambda i, j: (i, 0)),
                  pl.BlockSpec((1, 6, D_MODEL), lambda i, j: ((i * tm // seq) % nmod, 0, 0)),
                  pl.BlockSpec((D_MODEL, tf), lambda i, j: (0, j)),
                  pl.BlockSpec((D_MODEL, tf), lambda i, j: (0, nf + j)),
                  pl.BlockSpec((9, tf), lambda i, j: (0, j)),
                  pl.BlockSpec((1, tf), lambda i, j: (0, j)),
                  pl.BlockSpec((tf, D_MODEL), lambda i, j: (j, 0)),
                  full(prm['ln2_g']), full(prm['ln2_b'])],
        out_specs=pl.BlockSpec((tm, D_MODEL), lambda i, j: (i, 0)),
        out_shape=jax.ShapeDtypeStruct((tokens, D_MODEL), F32),
        scratch_shapes=[pltpu.VMEM((tm, D_MODEL), BF16), pltpu.VMEM((tm, D_MODEL), F32)],
        compiler_params=_cparams(("arbitrary", "arbitrary")),
    )(x2, mod, prm['w_ffn_up'], prm['w_ffn_up'], prm['ffn_conv_w'], prm['ffn_conv_b'], prm['w_ffn_down'],
      prm['ln2_g'], prm['ln2_b'])


def _layer(x, mod, s0, on_grid, prm):
    batch, seq, _ = x.shape
    tokens = batch * seq
    x2 = x.reshape(tokens, D_MODEL)
    p2 = _inproj(x2, mod, prm['w_in'], prm['conv_in'], seq)

    spec = _hyena_filter_spectrum(seq, prm['filt_w1'], prm['filt_b1'], prm['filt_w2'], prm['filt_b2'],
                                  prm['filt_w3'], prm['filt_freq'])
    yh = _hyena(p2.reshape(batch, seq, C_IN), spec, prm['hyena_bias'], seq)

    dec0, dec1, kd0, kd1, b0, b1, nkk, bonus, g = _rwkv_prep(p2, seq, prm)
    r2 = p2[:, 3 * D_HYENA:3 * D_HYENA + D_RWKV]
    v2 = p2[:, 3 * D_HYENA + 2 * D_RWKV:3 * D_HYENA + 3 * D_RWKV]
    lanes = 2 * batch * HEADS
    if s0 is None:
        s0l = jnp.zeros((HEAD, HEAD, lanes), F32)
    else:
        s0l = jnp.transpose(s0, (3, 4, 1, 0, 2)).reshape(HEAD, HEAD, lanes)
    y, st = _scan(_both(r2, r2, batch, seq), _both(dec0, dec1, batch, seq), _both(kd0, kd1, batch, seq),
                  _both(v2, v2, batch, seq), _both(nkk, nkk, batch, seq), _both(b0, b1, batch, seq), s0l)
    yd = jnp.transpose(y.reshape(seq, HEAD, 2, batch, HEADS), (2, 3, 0, 4, 1)).reshape(2, batch, seq, D_RWKV)
    y2 = (yd[0] + jnp.flip(yd[1], axis=1)).reshape(tokens, D_RWKV)
    s_final = jnp.transpose(st.reshape(HEAD, HEAD, 2, batch, HEADS), (3, 2, 4, 0, 1))

    x1 = _mix(y2, bonus, g, yh.reshape(tokens, D_HYENA), x2, mod, seq, prm)
    out = _ffn(x1, mod, seq, on_grid, prm)
    return out.reshape(batch, seq, D_MODEL), s_final


def kernel(x_prompt, x_sample, state_rwkv, c, c_ctx, ada_w, ada_b, w_in, conv_in, filt_w1, filt_b1, filt_w2, filt_b2, filt_w3, filt_freq, hyena_bias, rwkv_w0, rwkv_w_lora, rwkv_a0, rwkv_a_lora, rwkv_g_lora, rwkv_k_k, rwkv_k_a, rwkv_r_k, lnx_g, lnx_b, w_out, ln1_g, ln1_b, w_ffn_up, ffn_conv_w, ffn_conv_b, w_ffn_down, ln2_g, ln2_b):
    dec_batch = x_sample.shape[0]
    ones = jnp.asarray(_head_ones())
    y_prompt, y_sample = x_prompt, x_sample
    ctx_states = []
    for layer in range(DEPTH):
        row = lambda a: a[layer][None]
        prm = dict(
            ones=ones, w_in=w_in[layer].astype(BF16), conv_in=conv_in[layer],
            filt_w1=filt_w1[layer], filt_b1=filt_b1[layer], filt_w2=filt_w2[layer], filt_b2=filt_b2[layer],
            filt_w3=filt_w3[layer], filt_freq=filt_freq[layer], hyena_bias=row(hyena_bias),
            rwkv_w0=rwkv_w0[layer], rwkv_w_lora=rwkv_w_lora[layer], rwkv_a0=rwkv_a0[layer],
            rwkv_a_lora=rwkv_a_lora[layer], rwkv_g_lora=rwkv_g_lora[layer], rwkv_k_k=row(rwkv_k_k),
            rwkv_k_a=row(rwkv_k_a), rwkv_r_k=rwkv_r_k[layer].reshape(1, D_RWKV),
            lnx_g=row(lnx_g), lnx_b=row(lnx_b), w_out=w_out[layer].astype(BF16),
            ln1_g=row(ln1_g), ln1_b=row(ln1_b), w_ffn_up=w_ffn_up[layer].astype(BF16),
            ffn_conv_w=ffn_conv_w[layer].reshape(9, D_FF), ffn_conv_b=row(ffn_conv_b),
            w_ffn_down=w_ffn_down[layer].astype(BF16), ln2_g=row(ln2_g), ln2_b=row(ln2_b))
        pad = (-(dec_batch + 1)) % SUBLANES
        cond = jnp.concatenate([c, c_ctx[None], jnp.zeros((pad, D_MODEL), F32)], axis=0)
        mod = _ada(cond, ada_w[layer], row(ada_b)).reshape(cond.shape[0], 6, D_MODEL)
        y_prompt, s_ctx = _layer(y_prompt, mod[dec_batch:dec_batch + 1], None, False, prm)
        ctx_states.append(s_ctx)
        y_sample, _ = _layer(y_sample, mod[:dec_batch], state_rwkv[:, layer], True, prm)
    return (y_prompt, y_sample, jnp.stack(ctx_states, axis=1))
```

```python
import functools
import math

import ml_dtypes
import numpy as np
import jax
import jax.numpy as jnp
from jax import lax
from jax.experimental import pallas as pl
from jax.experimental.pallas import tpu as pltpu

F32 = jnp.float32
BF16 = jnp.bfloat16

D_MODEL = 1024
D_HYENA = 512
D_RWKV = 512
HEAD = 64
HEADS = D_RWKV // HEAD
LORA_W = 64
LORA_A = 64
LORA_G = 128
LORA_ALL = LORA_W + LORA_A + LORA_G
C_IN = 3 * D_HYENA + 3 * D_RWKV + LORA_ALL
FILT_BANDS = 16
FILT_FEAT = 1 + 2 * FILT_BANDS
FILT_HIDDEN = 64
N_FILT = 2 * D_HYENA
HYENA_TARGET = 1e-2
HYENA_FAST_PCT = 0.3
HYENA_SLOW_PCT = 1.5
D_FF = 2816
GRID_W = 64
DEPTH = 1
ALPHA = (2.0 * DEPTH) ** 0.25
LN_EPS = 1e-5
GN_EPS = 64e-5
NORM_EPS = 1e-12

LANES = 128
SUBLANES = 8
MXU_N = 256
ROW_TILE = 1024
SCAN_STEPS = 16
VMEM_LIMIT = 56 * 1024 * 1024


def _cparams(sem):
    return pltpu.CompilerParams(dimension_semantics=sem, vmem_limit_bytes=VMEM_LIMIT)


def _dot(a, b):
    return jnp.dot(a, b, preferred_element_type=F32)


def _split(x):
    hi = x.astype(BF16)
    lo = (x - hi.astype(F32)).astype(BF16)
    return hi, lo


def _dot_hi_const(x, c):
    hi, lo = _split(x)
    return _dot(hi, c) + _dot(lo, c)


def _dot3(ah, al, bh, bl):
    return _dot(ah, bh) + _dot(ah, bl) + _dot(al, bh)


def _layernorm_rows(x):
    mu = jnp.mean(x, axis=-1, keepdims=True)
    xc = x - mu
    var = jnp.mean(xc * xc, axis=-1, keepdims=True)
    return xc * lax.rsqrt(var + LN_EPS)


def _row_tile(tokens, seq):
    for tm in (ROW_TILE, 512, 256):
        if tokens % tm == 0 and (tm % seq == 0 or seq % tm == 0):
            return tm
    raise ValueError(f"no row tile for {tokens} tokens of sequence length {seq}")


def _np_split(x64):
    hi = x64.astype(ml_dtypes.bfloat16)
    lo = (x64 - hi.astype(np.float64)).astype(ml_dtypes.bfloat16)
    return hi, lo


@functools.lru_cache(maxsize=None)
def _dft_constants(L):
    N = 2 * L
    f = np.arange(L, dtype=np.int64)[:, None]
    t = np.arange(L, dtype=np.int64)[None, :]
    ang = (2.0 * np.pi / N) * ((f * t) % N).astype(np.float64)
    cos, sin = np.cos(ang), np.sin(ang)
    nyq = np.where(np.arange(L) % 2 == 0, 1.0, -1.0)
    fwd_s = sin.copy()
    fwd_s[0, :] = nyq
    fwd = np.concatenate([cos, fwd_s], axis=0)
    wf = np.full((L, 1), 2.0)
    wf[0, 0] = 1.0
    inv_c = (wf * cos).T / N
    inv_s = (2.0 * sin).T / N
    inv_s[:, 0] = nyq / N
    inv = np.concatenate([inv_c, inv_s], axis=1)
    return _np_split(fwd) + _np_split(inv)


@functools.lru_cache(maxsize=None)
def _filter_constants(L):
    t = np.arange(L, dtype=np.float64)[:, None] / L
    bands = np.arange(1, FILT_BANDS + 1, dtype=np.float64)[None, :]
    ang = (2.0 * math.pi) * bands * t
    feat = np.concatenate([t, np.sin(ang), np.cos(ang)], axis=-1)
    feat = np.pad(feat, ((0, 0), (0, FILT_HIDDEN - FILT_FEAT)))
    slow = abs(math.log(HYENA_TARGET) / HYENA_SLOW_PCT)
    fast = abs(math.log(HYENA_TARGET) / HYENA_FAST_PCT)
    deltas = np.linspace(slow, fast, N_FILT, dtype=np.float64)
    window = np.exp(-t * deltas[None, :])
    return feat.astype(np.float32), window.astype(np.float32)


@functools.lru_cache(maxsize=None)
def _head_ones():
    h = np.arange(D_RWKV) // HEAD
    return (h[:, None] == h[None, :]).astype(ml_dtypes.bfloat16)


def _ada_kernel(c_ref, w_ref, b_ref, o_ref):
    c = c_ref[...]
    s = c * jax.nn.sigmoid(c)
    o_ref[...] = _dot(s.astype(BF16), w_ref[...].astype(BF16)) + b_ref[...]


def _ada(cond, ada_w, ada_b):
    rows = cond.shape[0]
    tn = 1024
    return pl.pallas_call(
        _ada_kernel,
        grid=(6 * D_MODEL // tn,),
        in_specs=[pl.BlockSpec((rows, D_MODEL), lambda j: (0, 0)),
                  pl.BlockSpec((D_MODEL, tn), lambda j: (0, j)),
                  pl.BlockSpec((1, tn), lambda j: (0, j))],
        out_specs=pl.BlockSpec((rows, tn), lambda j: (0, j)),
        out_shape=jax.ShapeDtypeStruct((rows, 6 * D_MODEL), F32),
        compiler_params=_cparams(("arbitrary",)),
        name="ada_mod",
    )(cond, ada_w, ada_b)


def _inproj_kernel(x_ref, mod_ref, w_ref, cw_ref, o_ref, h_ref, *, seq):
    @pl.when(pl.program_id(1) == 0)
    def _():
        m = mod_ref[0]
        h = _layernorm_rows(x_ref[...]) * (1.0 + m[1:2]) + m[0:1]
        h_ref[...] = h.astype(BF16)

    p = _dot(h_ref[...], w_ref[...])
    rows = p.shape[0]
    t = lax.broadcasted_iota(jnp.int32, p.shape, 0) & (seq - 1)
    prev = jnp.where(t == 0, 0.0, pltpu.roll(p, 1, 0))
    nxt = jnp.where(t == seq - 1, 0.0, pltpu.roll(p, rows - 1, 0))
    cw = cw_ref[...]
    o_ref[...] = cw[0:1] * prev + cw[1:2] * p + cw[2:3] * nxt


def _inproj(x2, mod, w_in_bf, conv_in, seq):
    tokens = x2.shape[0]
    tm = _row_tile(tokens, seq)
    nmod = mod.shape[0]
    assert nmod == 1 or tm <= seq
    tn = MXU_N
    return pl.pallas_call(
        functools.partial(_inproj_kernel, seq=seq),
        grid=(tokens // tm, C_IN // tn),
        in_specs=[pl.BlockSpec((tm, D_MODEL), lambda i, j: (i, 0)),
                  pl.BlockSpec((1, 6, D_MODEL), lambda i, j: ((i * tm // seq) % nmod, 0, 0)),
                  pl.BlockSpec((D_MODEL, tn), lambda i, j: (0, j)),
                  pl.BlockSpec((3, tn), lambda i, j: (0, j))],
        out_specs=pl.BlockSpec((tm, tn), lambda i, j: (i, j)),
        out_shape=jax.ShapeDtypeStruct((tokens, C_IN), F32),
        scratch_shapes=[pltpu.VMEM((tm, D_MODEL), BF16)],
        compiler_params=_cparams(("arbitrary", "arbitrary")),
        name="inproj_conv",
    )(x2, mod, w_in_bf, conv_in)


def _filter_kernel(feat_ref, win_ref, w1_ref, b1_ref, w2_ref, b2_ref, w3_ref, fr_ref,
                   fh_ref, fl_ref, a_ref, an_ref, b_ref, *, seq):
    def hp_dot(x, w):
        xh, xl = _split(x)
        wh, wl = _split(w)
        return _dot3(xh, xl, wh, wl)

    fr = fr_ref[...]
    h = jnp.sin(fr[0:1] * (hp_dot(feat_ref[...], w1_ref[...]) + b1_ref[...]))
    h = jnp.sin(fr[1:2] * (hp_dot(h, w2_ref[...]) + b2_ref[...]))
    h = hp_dot(h, w3_ref[...]) * win_ref[...]
    col = jnp.sum(jnp.abs(h), axis=0, keepdims=True)
    inv = 1.0 / (col[:, :D_HYENA] + col[:, D_HYENA:])
    hf = h[:, :D_HYENA] * inv
    hb = h[:, D_HYENA:] * inv
    row = lax.broadcasted_iota(jnp.int32, hf.shape, 0)
    hb = jnp.where(row == 0, 0.0, hb)
    eh, el = _split(hf + hb)
    oh, ol = _split(hf - hb)
    fh, fl = fh_ref[...], fl_ref[...]
    fe = _dot3(fh, fl, eh, el)
    fo = _dot3(fh[seq:], fl[seq:], oh, ol)
    a = fe[:seq]
    a_ref[...] = a
    an_ref[...] = jnp.where(row == 0, fe[seq:seq + 1], a)
    b_ref[...] = jnp.where(row == 0, 0.0, fo)


def _hyena_filter_spectrum(seq, w1, b1, w2, b2, w3, freq):
    feat, window = _filter_constants(seq)
    fh, fl, _, _ = _dft_constants(seq)
    w1p = jnp.pad(w1, ((0, FILT_HIDDEN - FILT_FEAT), (0, 0)))
    shp = jax.ShapeDtypeStruct((seq, D_HYENA), F32)
    return pl.pallas_call(
        functools.partial(_filter_kernel, seq=seq),
        out_shape=(shp, shp, shp),
        compiler_params=pltpu.CompilerParams(vmem_limit_bytes=VMEM_LIMIT),
        name="hyena_filter",
    )(feat, window, w1p, b1[None], w2, b2[None], w3, freq, fh, fl)


def _hyena_kernel(hv_ref, hx0_ref, hx1_ref, fh_ref, fl_ref, gh_ref, gl_ref, a_ref, an_ref, b_ref,
                  bias_ref, o_ref, *, seq):
    z = hx1_ref[...] * hv_ref[...]
    zh, zl = _split(z)
    zf = _dot3(fh_ref[...], fl_ref[...], zh, zl)
    zc, zs = zf[:seq], zf[seq:]
    a, an, b = a_ref[...], an_ref[...], b_ref[...]
    yc = zc * a - zs * b
    ys = zc * b + zs * an
    yh, yl = _split(jnp.concatenate([yc, ys], axis=0))
    y = _dot3(gh_ref[...], gl_ref[...], yh, yl)
    o_ref[...] = (hx0_ref[...] * (y + bias_ref[...] * z)).astype(o_ref.dtype)


def _hyena(p3, spec, bias, seq):
    batch = p3.shape[0]
    fh, fl, gh, gl = _dft_constants(seq)
    a, an, b = spec
    cb = MXU_N
    ncb = D_HYENA // cb
    const = lambda shape: pl.BlockSpec(shape, lambda c, i: (0, 0), pipeline_mode=pl.Buffered(1))
    col = lambda k: pl.BlockSpec((None, seq, cb), lambda c, i: (i, 0, k * ncb + c))
    chan = lambda rows: pl.BlockSpec((rows, cb), lambda c, i: (0, c))
    return pl.pallas_call(
        functools.partial(_hyena_kernel, seq=seq),
        grid=(ncb, batch),
        in_specs=[col(0), col(1), col(2),
                  const((2 * seq, seq)), const((2 * seq, seq)), const((seq, 2 * seq)), const((seq, 2 * seq)),
                  chan(seq), chan(seq), chan(seq), chan(1)],
        out_specs=pl.BlockSpec((None, seq, cb), lambda c, i: (i, 0, c)),
        out_shape=jax.ShapeDtypeStruct((batch, seq, D_HYENA), BF16),
        compiler_params=_cparams(("arbitrary", "arbitrary")),
        name="hyena_conv",
    )(p3, p3, p3, fh, fl, gh, gl, a, an, b, bias)


PAIR_TILE = 256


def _store_paired(x, xs_ref, put):
    rows = x.shape[0]
    for hp in range(HEADS // 2):
        xs_ref[hp] = x[:, hp * LANES:(hp + 1) * LANES]
    low = lax.broadcasted_iota(jnp.int32, (rows // 2, LANES), 1) < HEAD
    for hp in range(HEADS // 2):
        even = xs_ref[hp, pl.ds(0, rows // 2, stride=2), :]
        odd = xs_ref[hp, pl.ds(1, rows // 2, stride=2), :]
        put(2 * hp, jnp.where(low, even, pltpu.roll(odd, HEAD, 1)))
        put(2 * hp + 1, jnp.where(low, pltpu.roll(even, HEAD, 1), odd))


def _load_paired(get, xs_ref):
    rows = xs_ref.shape[1]
    low = lax.broadcasted_iota(jnp.int32, (rows // 2, LANES), 1) < HEAD
    for hp in range(HEADS // 2):
        b0, b1 = get(2 * hp), get(2 * hp + 1)
        xs_ref[hp, pl.ds(0, rows // 2, stride=2), :] = jnp.where(low, b0, pltpu.roll(b1, HEAD, 1))
        xs_ref[hp, pl.ds(1, rows // 2, stride=2), :] = jnp.where(low, pltpu.roll(b0, HEAD, 1), b1)
    return jnp.concatenate([xs_ref[hp] for hp in range(HEADS // 2)], axis=1)


def _rwkv_prep_kernel(r_ref, k_ref, v_ref, lora_ref, ones_ref, w0_ref, wl_ref, a0_ref, al_ref, gl_ref,
                      kk_ref, ka_ref, rk_ref,
                      rs_ref, vs_ref, as_ref, wd_ref, kd_ref, bd_ref, bonus_ref, g_ref, xs_ref):
    r, k, v = r_ref[...], k_ref[...], v_ref[...]
    lora = lora_ref[...]
    wd = lora[:, :LORA_W]
    ad = lora[:, LORA_W:LORA_W + LORA_A]
    gd = lora[:, LORA_W + LORA_A:]
    ones = ones_ref[...]

    def shared(ref):
        def put(h, val):
            ref[h] = val
        return put

    def directed(ref, d):
        def put(h, val):
            ref[d, h] = val
        return put

    g_ref[...] = _dot(jax.nn.sigmoid(gd).astype(BF16), gl_ref[...].astype(BF16))
    kk = k * kk_ref[...]
    norm = jnp.sqrt(_dot_hi_const(kk * kk, ones))
    kk = kk / jnp.maximum(norm, NORM_EPS)
    _store_paired(r, xs_ref, shared(rs_ref))
    _store_paired(v, xs_ref, shared(vs_ref))
    _store_paired(-kk, xs_ref, shared(as_ref))
    tw = jnp.tanh(wd).astype(BF16)
    adb = ad.astype(BF16)
    ksum = jnp.zeros_like(k)
    for d in range(2):
        w_raw = w0_ref[d:d + 1] + _dot(tw, wl_ref[d].astype(BF16))
        _store_paired(jnp.exp(-math.exp(-0.5) * jax.nn.sigmoid(w_raw)), xs_ref, directed(wd_ref, d))
        a = jax.nn.sigmoid(a0_ref[d:d + 1] + _dot(adb, al_ref[d].astype(BF16)))
        kd = k * (1.0 + (a - 1.0) * ka_ref[...])
        _store_paired(kd, xs_ref, directed(kd_ref, d))
        _store_paired(kk * a, xs_ref, directed(bd_ref, d))
        ksum = ksum + kd
    bonus_ref[...] = _dot_hi_const(r * ksum * rk_ref[...], ones) * v


def _rwkv_prep(p3, prm):
    batch, seq, _ = p3.shape
    tm = PAIR_TILE
    cbase = 3 * D_HYENA // D_RWKV
    col = lambda k: pl.BlockSpec((None, tm, D_RWKV), lambda b, i: (b, i, cbase + k))
    full = lambda a: pl.BlockSpec(a.shape, lambda b, i: (0,) * a.ndim)
    params = [prm['ones'], prm['rwkv_w0'], prm['rwkv_w_lora'], prm['rwkv_a0'], prm['rwkv_a_lora'],
              prm['rwkv_g_lora'], prm['rwkv_k_k'], prm['rwkv_k_a'], prm['rwkv_r_k']]
    shared = jax.ShapeDtypeStruct((batch, HEADS, seq // 2, LANES), F32)
    directed = jax.ShapeDtypeStruct((2, batch, HEADS, seq // 2, LANES), F32)
    rows = jax.ShapeDtypeStruct((batch, seq, D_RWKV), F32)
    shared_spec = pl.BlockSpec((None, HEADS, tm // 2, LANES), lambda b, i: (b, 0, i, 0))
    directed_spec = pl.BlockSpec((2, None, HEADS, tm // 2, LANES), lambda b, i: (0, b, 0, i, 0))
    rows_spec = pl.BlockSpec((None, tm, D_RWKV), lambda b, i: (b, i, 0))
    return pl.pallas_call(
        _rwkv_prep_kernel,
        grid=(batch, seq // tm),
        in_specs=[col(0), col(1), col(2),
                  pl.BlockSpec((None, tm, LORA_ALL), lambda b, i: (b, i, (C_IN - LORA_ALL) // LORA_ALL))]
                 + [full(a) for a in params],
        out_specs=[shared_spec] * 3 + [directed_spec] * 3 + [rows_spec] * 2,
        out_shape=(shared,) * 3 + (directed,) * 3 + (rows,) * 2,
        scratch_shapes=[pltpu.VMEM((HEADS // 2, tm, LANES), F32)],
        compiler_params=_cparams(("arbitrary", "arbitrary")),
        name="rwkv_prep",
    )(p3, p3, p3, p3, *params)


N_SCAN_OPERANDS = 6
_V = 3


def _scan_kernel(*refs, mixed, m_tiles):
    nq = N_SCAN_OPERANDS
    if mixed:
        fwd, bwd, s0_ref = refs[:nq], refs[nq:2 * nq], refs[2 * nq]
        yf_ref, yb_ref, st_ref, s_ref, in_ref, yo_ref = refs[2 * nq + 1:]
    else:
        fwd, s0_ref = refs[:nq], refs[nq]
        y_ref, st_ref, s_ref, in_ref, yo_ref = refs[nq + 1:]
    npair = SCAN_STEPS // 2
    tc = pl.program_id(1)
    backward = pl.program_id(0) >= m_tiles

    @pl.when(tc == 0)
    def _():
        s_ref[...] = s0_ref[...]

    low = lax.broadcasted_iota(jnp.int32, (HEAD, LANES), 1) < HEAD

    def step(sidx):
        def vblock(vb, c2):
            base = pl.multiple_of(vb * SUBLANES, SUBLANES)
            r, w, kk = in_ref[sidx, 0], in_ref[sidx, 1], in_ref[sidx, 2]
            a, b = in_ref[sidx, 4], in_ref[sidx, 5]
            v8 = in_ref[sidx, _V, pl.ds(base, SUBLANES), :]
            ys = []
            for i in range(SUBLANES):
                sv = s_ref[base + i]
                sa = jnp.sum(sv * a, axis=0, keepdims=True)
                sn = sv * w + sa * b + v8[i:i + 1, :] * kk
                s_ref[base + i] = sn
                ys.append(jnp.sum(sn * r, axis=0, keepdims=True))
            yo_ref[pl.ds(sidx * HEAD + base, SUBLANES), :] = jnp.concatenate(ys, axis=0)
            return c2

        lax.fori_loop(0, HEAD // SUBLANES, vblock, 0)

    def pair(jj, carry):
        src_b = npair - 1 - jj
        src = jnp.where(backward, src_b, jj)
        for q in range(nq):
            if mixed:
                t = jnp.concatenate([fwd[q][:, jj, :], bwd[q][:, src_b, :]], axis=0).T
                in_ref[0, q] = jnp.where(low, t[:HEAD], t[HEAD:])
                in_ref[1, q] = jnp.where(low, t[HEAD:], t[:HEAD])
            else:
                t = fwd[q][:, src, :].T
                in_ref[0, q] = jnp.where(backward, t[HEAD:], t[:HEAD])
                in_ref[1, q] = jnp.where(backward, t[:HEAD], t[HEAD:])
        step(0)
        step(1)
        z = yo_ref[...].T
        if mixed:
            yf_ref[:, jj, :] = z[:HEAD]
            yb_ref[:, src_b, :] = pltpu.roll(z[HEAD:], HEAD, 1)
        else:
            y_ref[:, src, :] = jnp.where(backward, pltpu.roll(z, HEAD, 1), z)
        return carry

    lax.fori_loop(0, npair, pair, 0)

    @pl.when(tc == pl.num_programs(1) - 1)
    def _():
        st_ref[...] = s_ref[...]


def _scan(shared, directed, s0):
    rs, vs, as_ = shared
    wd, kd, bd = directed
    half, hseq, _ = rs.shape
    npair = SCAN_STEPS // 2
    n = hseq // npair
    mixed = half < LANES
    st = pl.BlockSpec((HEAD, HEAD, LANES), lambda l, t: (0, 0, l))
    scratch = [pltpu.VMEM((HEAD, HEAD, LANES), F32), pltpu.VMEM((2, N_SCAN_OPERANDS, HEAD, LANES), F32),
               pltpu.VMEM((2 * HEAD, LANES), F32)]
    st_shape = jax.ShapeDtypeStruct((HEAD, HEAD, 2 * half), F32)
    if mixed:
        assert 2 * half == LANES
        blk = (half, npair, LANES)
        sf = pl.BlockSpec(blk, lambda l, t: (0, t, 0))
        sb = pl.BlockSpec(blk, lambda l, t: (0, n - 1 - t, 0))
        df = pl.BlockSpec((None,) + blk, lambda l, t: (0, 0, t, 0))
        db = pl.BlockSpec((None,) + blk, lambda l, t: (1, 0, n - 1 - t, 0))
        yshape = jax.ShapeDtypeStruct((half, hseq, LANES), F32)
        yf, yb, stt = pl.pallas_call(
            functools.partial(_scan_kernel, mixed=True, m_tiles=1),
            grid=(1, n),
            in_specs=[sf, df, df, sf, sf, df, sb, db, db, sb, sb, db, st],
            out_specs=[sf, sb, st],
            out_shape=(yshape, yshape, st_shape),
            scratch_shapes=scratch,
            compiler_params=_cparams(("arbitrary", "arbitrary")),
            name="rwkv_scan_mixed",
        )(rs, wd, kd, vs, as_, bd, rs, wd, kd, vs, as_, bd, s0)
        return yf, yb, 0, 0, stt
    assert half % LANES == 0
    m = half // LANES
    chunk = lambda l, t: jnp.where(l >= m, n - 1 - t, t)
    blk = (LANES, npair, LANES)
    sp = pl.BlockSpec(blk, lambda l, t: (l % m, chunk(l, t), 0))
    dp = pl.BlockSpec((None,) + blk, lambda l, t: (l // m, l % m, chunk(l, t), 0))
    y, stt = pl.pallas_call(
        functools.partial(_scan_kernel, mixed=False, m_tiles=m),
        grid=(2 * m, n),
        in_specs=[sp, dp, dp, sp, sp, dp, st],
        out_specs=[dp, st],
        out_shape=(jax.ShapeDtypeStruct((2, half, hseq, LANES), F32), st_shape),
        scratch_shapes=scratch,
        compiler_params=_cparams(("arbitrary", "arbitrary")),
        name="rwkv_scan",
    )(rs, wd, kd, vs, as_, bd, s0)
    y2 = y.reshape(2 * half, hseq, LANES)
    return y2, y2, 0, half, stt


def _mix_kernel(yf_ref, yb_ref, bonus_ref, g_ref, yh_ref, x_ref, mod_ref, ones_ref, wo_ref, lng_ref, lnb_ref,
                l1g_ref, l1b_ref, o_ref, xs_ref):
    ones = ones_ref[...]
    y = _load_paired(lambda h: yf_ref[h] + yb_ref[h], xs_ref)
    mu = _dot_hi_const(y, ones) * (1.0 / HEAD)
    yc = y - mu
    var = _dot_hi_const(yc * yc, ones) * (1.0 / HEAD)
    yr = yc * lax.rsqrt(var + GN_EPS) * lng_ref[...] + lnb_ref[...] + bonus_ref[...]
    yr = (yr * g_ref[...]).astype(BF16)
    mix = _dot(yh_ref[...], wo_ref[:D_HYENA, :]) + _dot(yr, wo_ref[D_HYENA:, :])
    m = mod_ref[0]
    z = _layernorm_rows(ALPHA * x_ref[...] + m[2:3] * mix)
    o_ref[...] = z * l1g_ref[...] + l1b_ref[...]


def _mix(yf, yb, off_f, off_b, bonus, g, yh, x, mod, prm):
    batch, seq, _ = x.shape
    tm = PAIR_TILE
    nmod = mod.shape[0]
    yf4 = yf.reshape(-1, HEADS, seq // 2, LANES)
    yb4 = yb.reshape(-1, HEADS, seq // 2, LANES)
    bf, bb = off_f // HEADS, off_b // HEADS
    row = lambda width: pl.BlockSpec((None, tm, width), lambda b, i: (b, i, 0))
    full = lambda a: pl.BlockSpec(a.shape, lambda b, i: (0,) * a.ndim)
    params = [prm['ones'], prm['w_out'], prm['lnx_g'], prm['lnx_b'], prm['ln1_g'], prm['ln1_b']]
    return pl.pallas_call(
        _mix_kernel,
        grid=(batch, seq // tm),
        in_specs=[pl.BlockSpec((None, HEADS, tm // 2, LANES), lambda b, i: (bf + b, 0, i, 0)),
                  pl.BlockSpec((None, HEADS, tm // 2, LANES), lambda b, i: (bb + b, 0, i, 0)),
                  row(D_RWKV), row(D_RWKV), row(D_HYENA), row(D_MODEL),
                  pl.BlockSpec((1, 6, D_MODEL), lambda b, i: (b % nmod, 0, 0))]
                 + [full(a) for a in params],
        out_specs=row(D_MODEL),
        out_shape=jax.ShapeDtypeStruct((batch, seq, D_MODEL), F32),
        scratch_shapes=[pltpu.VMEM((HEADS // 2, tm, LANES), F32)],
        compiler_params=_cparams(("arbitrary", "arbitrary")),
        name="mix_outproj",
    )(yf4, yb4, bonus, g, yh, x, mod, *params)


def _ffn_kernel(x_ref, mod_ref, wu_ref, wg_ref, cw_ref, cb_ref, wd_ref, l2g_ref, l2b_ref, o_ref,
                h_ref, acc_ref, *, seq, on_grid):
    j = pl.program_id(1)

    @pl.when(j == 0)
    def _():
        m = mod_ref[0]
        h = _layernorm_rows(x_ref[...]) * (1.0 + m[4:5]) + m[3:4]
        h_ref[...] = h.astype(BF16)
        acc_ref[...] = jnp.zeros_like(acc_ref)

    h = h_ref[...]
    u = _dot(h, wu_ref[...])
    gt = _dot(h, wg_ref[...])
    rows = u.shape[0]
    t = lax.broadcasted_iota(jnp.int32, u.shape, 0)
    cw = cw_ref[...]
    if on_grid:
        col = t & (GRID_W - 1)
        grow = (t & (seq - 1)) >> int(math.log2(GRID_W))
        um = jnp.where(col == 0, 0.0, pltpu.roll(u, 1, 0))
        up = jnp.where(col == GRID_W - 1, 0.0, pltpu.roll(u, rows - 1, 0))
        hrow = [cw[3 * di:3 * di + 1] * um + cw[3 * di + 1:3 * di + 2] * u + cw[3 * di + 2:3 * di + 3] * up
                for di in range(3)]
        u = (hrow[1]
             + jnp.where(grow == 0, 0.0, pltpu.roll(hrow[0], GRID_W, 0))
             + jnp.where(grow == seq // GRID_W - 1, 0.0, pltpu.roll(hrow[2], rows - GRID_W, 0)))
    else:
        ts = t & (seq - 1)
        um = jnp.where(ts == 0, 0.0, pltpu.roll(u, 1, 0))
        up = jnp.where(ts == seq - 1, 0.0, pltpu.roll(u, rows - 1, 0))
        u = cw[3:4] * um + cw[4:5] * u + cw[5:6] * up
    u = u + cb_ref[...]
    act = 0.5 * u * (1.0 + jnp.tanh(math.sqrt(2.0 / math.pi) * (u + 0.044715 * (u * u * u))))
    acc_ref[...] += _dot((act * gt).astype(BF16), wd_ref[...])

    @pl.when(j == pl.num_programs(1) - 1)
    def _():
        m = mod_ref[0]
        z = _layernorm_rows(ALPHA * x_ref[...] + m[5:6] * acc_ref[...])
        o_ref[...] = z * l2g_ref[...] + l2b_ref[...]


def _ffn(x2, mod, seq, on_grid, prm):
    tokens = x2.shape[0]
    tm = _row_tile(tokens, seq)
    nmod = mod.shape[0]
    assert nmod == 1 or tm <= seq
    assert not on_grid or tm % seq == 0
    tf = MXU_N
    nf = D_FF // tf
    full = lambda a: pl.BlockSpec(a.shape, lambda i, j: (0,) * a.ndim)
    return pl.pallas_call(
        functools.partial(_ffn_kernel, seq=seq, on_grid=on_grid),
        grid=(tokens // tm, nf),
        in_specs=[pl.BlockSpec((tm, D_MODEL), lambda i, j: (i, 0)),
                  pl.BlockSpec((1, 6, D_MODEL), lambda i, j: ((i * tm // seq) % nmod, 0, 0)),
                  pl.BlockSpec((D_MODEL, tf), lambda i, j: (0, j)),
                  pl.BlockSpec((D_MODEL, tf), lambda i, j: (0, nf + j)),
                  pl.BlockSpec((9, tf), lambda i, j: (0, j)),
                  pl.BlockSpec((1, tf), lambda i, j: (0, j)),
                  pl.BlockSpec((tf, D_MODEL), lambda i, j: (j, 0)),
                  full(prm['ln2_g']), full(prm['ln2_b'])],
        out_specs=pl.BlockSpec((tm, D_MODEL), lambda i, j: (i, 0)),
        out_shape=jax.ShapeDtypeStruct((tokens, D_MODEL), F32),
        scratch_shapes=[pltpu.VMEM((tm, D_MODEL), BF16), pltpu.VMEM((tm, D_MODEL), F32)],
        compiler_params=_cparams(("arbitrary", "arbitrary")),
        name="conv_ffn",
    )(x2, mod, prm['w_ffn_up'], prm['w_ffn_up'], prm['ffn_conv_w'], prm['ffn_conv_b'], prm['w_ffn_down'],
      prm['ln2_g'], prm['ln2_b'])


def _layer(x, mod, s0, on_grid, prm):
    batch, seq, _ = x.shape
    tokens = batch * seq
    p2 = _inproj(x.reshape(tokens, D_MODEL), mod, prm['w_in'], prm['conv_in'], seq)
    p3 = p2.reshape(batch, seq, C_IN)

    spec = _hyena_filter_spectrum(seq, prm['filt_w1'], prm['filt_b1'], prm['filt_w2'], prm['filt_b2'],
                                  prm['filt_w3'], prm['filt_freq'])
    yh = _hyena(p3, spec, prm['hyena_bias'], seq)

    rs, vs, as_, wd, kd, bd, bonus, g = _rwkv_prep(p3, prm)
    half = batch * HEADS
    flat = lambda a: a.reshape(a.shape[:-4] + (half, seq // 2, LANES))
    if s0 is None:
        s0l = jnp.zeros((HEAD, HEAD, 2 * half), F32)
    else:
        s0l = jnp.transpose(s0, (3, 4, 1, 0, 2)).reshape(HEAD, HEAD, 2 * half)
    yf, yb, off_f, off_b, st = _scan((flat(rs), flat(vs), flat(as_)), (flat(wd), flat(kd), flat(bd)), s0l)
    s_final = jnp.transpose(st.reshape(HEAD, HEAD, 2, batch, HEADS), (3, 2, 4, 0, 1))

    x1 = _mix(yf, yb, off_f, off_b, bonus, g, yh, x, mod, prm)
    out = _ffn(x1.reshape(tokens, D_MODEL), mod, seq, on_grid, prm)
    return out.reshape(batch, seq, D_MODEL), s_final


def kernel(x_prompt, x_sample, state_rwkv, c, c_ctx, ada_w, ada_b, w_in, conv_in, filt_w1, filt_b1, filt_w2, filt_b2, filt_w3, filt_freq, hyena_bias, rwkv_w0, rwkv_w_lora, rwkv_a0, rwkv_a_lora, rwkv_g_lora, rwkv_k_k, rwkv_k_a, rwkv_r_k, lnx_g, lnx_b, w_out, ln1_g, ln1_b, w_ffn_up, ffn_conv_w, ffn_conv_b, w_ffn_down, ln2_g, ln2_b):
    dec_batch = x_sample.shape[0]
    ones = jnp.asarray(_head_ones())
    y_prompt, y_sample = x_prompt, x_sample
    ctx_states = []
    for layer in range(DEPTH):
        row = lambda a: a[layer][None]
        prm = dict(
            ones=ones, w_in=w_in[layer].astype(BF16), conv_in=conv_in[layer],
            filt_w1=filt_w1[layer], filt_b1=filt_b1[layer], filt_w2=filt_w2[layer], filt_b2=filt_b2[layer],
            filt_w3=filt_w3[layer], filt_freq=filt_freq[layer], hyena_bias=row(hyena_bias),
            rwkv_w0=rwkv_w0[layer], rwkv_w_lora=rwkv_w_lora[layer], rwkv_a0=rwkv_a0[layer],
            rwkv_a_lora=rwkv_a_lora[layer], rwkv_g_lora=rwkv_g_lora[layer], rwkv_k_k=row(rwkv_k_k),
            rwkv_k_a=row(rwkv_k_a), rwkv_r_k=rwkv_r_k[layer].reshape(1, D_RWKV),
            lnx_g=row(lnx_g), lnx_b=row(lnx_b), w_out=w_out[layer].astype(BF16),
            ln1_g=row(ln1_g), ln1_b=row(ln1_b), w_ffn_up=w_ffn_up[layer].astype(BF16),
            ffn_conv_w=ffn_conv_w[layer].reshape(9, D_FF), ffn_conv_b=row(ffn_conv_b),
            w_ffn_down=w_ffn_down[layer].astype(BF16), ln2_g=row(ln2_g), ln2_b=row(ln2_b))
        pad = (-(dec_batch + 1)) % SUBLANES
        cond = jnp.concatenate([c, c_ctx[None], jnp.zeros((pad, D_MODEL), F32)], axis=0)
        mod = _ada(cond, ada_w[layer], row(ada_b)).reshape(cond.shape[0], 6, D_MODEL)
        y_prompt, s_ctx = _layer(y_prompt, mod[dec_batch:dec_batch + 1], None, False, prm)
        ctx_states.append(s_ctx)
        y_sample, _ = _layer(y_sample, mod[:dec_batch], state_rwkv[:, layer], True, prm)
    return (y_prompt, y_sample, jnp.stack(ctx_states, axis=1))
```

```python
import functools
import math

import ml_dtypes
import numpy as np
import jax
import jax.numpy as jnp
from jax import lax
from jax.experimental import pallas as pl
from jax.experimental.pallas import tpu as pltpu

F32 = jnp.float32
BF16 = jnp.bfloat16

D_MODEL = 1024
D_HYENA = 512
D_RWKV = 512
HEAD = 64
HEADS = D_RWKV // HEAD
LORA_W = 64
LORA_A = 64
LORA_G = 128
LORA_ALL = LORA_W + LORA_A + LORA_G
C_IN = 3 * D_HYENA + 3 * D_RWKV + LORA_ALL
FILT_BANDS = 16
FILT_FEAT = 1 + 2 * FILT_BANDS
FILT_HIDDEN = 64
N_FILT = 2 * D_HYENA
HYENA_TARGET = 1e-2
HYENA_FAST_PCT = 0.3
HYENA_SLOW_PCT = 1.5
D_FF = 2816
GRID_W = 64
DEPTH = 1
ALPHA = (2.0 * DEPTH) ** 0.25
LN_EPS = 1e-5
GN_EPS = 64e-5
NORM_EPS = 1e-12

LANES = 128
SUBLANES = 8
MXU_N = 256
ROW_TILE = 1024
SCAN_STEPS = 16
VMEM_LIMIT = 56 * 1024 * 1024


def _cparams(sem):
    return pltpu.CompilerParams(dimension_semantics=sem, vmem_limit_bytes=VMEM_LIMIT)


def _dot(a, b):
    return jnp.dot(a, b, preferred_element_type=F32)


def _split(x):
    hi = x.astype(BF16)
    lo = (x - hi.astype(F32)).astype(BF16)
    return hi, lo


def _dot_hi_const(x, c):
    hi, lo = _split(x)
    return _dot(hi, c) + _dot(lo, c)


def _dot3(ah, al, bh, bl):
    return _dot(ah, bh) + _dot(ah, bl) + _dot(al, bh)


def _layernorm_rows(x):
    mu = jnp.mean(x, axis=-1, keepdims=True)
    xc = x - mu
    var = jnp.mean(xc * xc, axis=-1, keepdims=True)
    return xc * lax.rsqrt(var + LN_EPS)


def _row_tile(tokens, seq):
    for tm in (ROW_TILE, 512, 256):
        if tokens % tm == 0 and (tm % seq == 0 or seq % tm == 0):
            return tm
    raise ValueError(f"no row tile for {tokens} tokens of sequence length {seq}")


def _np_split(x64):
    hi = x64.astype(ml_dtypes.bfloat16)
    lo = (x64 - hi.astype(np.float64)).astype(ml_dtypes.bfloat16)
    return hi, lo


@functools.lru_cache(maxsize=None)
def _dft_constants(L):
    N = 2 * L
    f = np.arange(L, dtype=np.int64)[:, None]
    t = np.arange(L, dtype=np.int64)[None, :]
    ang = (2.0 * np.pi / N) * ((f * t) % N).astype(np.float64)
    cos, sin = np.cos(ang), np.sin(ang)
    nyq = np.where(np.arange(L) % 2 == 0, 1.0, -1.0)
    fwd_s = sin.copy()
    fwd_s[0, :] = nyq
    fwd = np.concatenate([cos, fwd_s], axis=0)
    wf = np.full((L, 1), 2.0)
    wf[0, 0] = 1.0
    inv_c = (wf * cos).T / N
    inv_s = (2.0 * sin).T / N
    inv_s[:, 0] = nyq / N
    inv = np.concatenate([inv_c, inv_s], axis=1)
    return _np_split(fwd) + _np_split(inv)


@functools.lru_cache(maxsize=None)
def _filter_constants(L):
    t = np.arange(L, dtype=np.float64)[:, None] / L
    bands = np.arange(1, FILT_BANDS + 1, dtype=np.float64)[None, :]
    ang = (2.0 * math.pi) * bands * t
    feat = np.concatenate([t, np.sin(ang), np.cos(ang)], axis=-1)
    feat = np.pad(feat, ((0, 0), (0, FILT_HIDDEN - FILT_FEAT)))
    slow = abs(math.log(HYENA_TARGET) / HYENA_SLOW_PCT)
    fast = abs(math.log(HYENA_TARGET) / HYENA_FAST_PCT)
    deltas = np.linspace(slow, fast, N_FILT, dtype=np.float64)
    window = np.exp(-t * deltas[None, :])
    return feat.astype(np.float32), window.astype(np.float32)


@functools.lru_cache(maxsize=None)
def _head_ones():
    h = np.arange(D_RWKV) // HEAD
    return (h[:, None] == h[None, :]).astype(ml_dtypes.bfloat16)


def _ada_kernel(c_ref, w_ref, b_ref, o_ref):
    c = c_ref[...]
    s = c * jax.nn.sigmoid(c)
    o_ref[...] = _dot(s.astype(BF16), w_ref[...].astype(BF16)) + b_ref[...]


def _ada(cond, ada_w, ada_b):
    rows = cond.shape[0]
    tn = 1024
    return pl.pallas_call(
        _ada_kernel,
        grid=(6 * D_MODEL // tn,),
        in_specs=[pl.BlockSpec((rows, D_MODEL), lambda j: (0, 0)),
                  pl.BlockSpec((D_MODEL, tn), lambda j: (0, j)),
                  pl.BlockSpec((1, tn), lambda j: (0, j))],
        out_specs=pl.BlockSpec((rows, tn), lambda j: (0, j)),
        out_shape=jax.ShapeDtypeStruct((rows, 6 * D_MODEL), F32),
        compiler_params=_cparams(("arbitrary",)),
        name="ada_mod",
    )(cond, ada_w, ada_b)


def _inproj_kernel(x_ref, mod_ref, w_ref, cw_ref, o_ref, h_ref, mask_ref, *, seq):
    @pl.when(pl.program_id(1) == 0)
    def _():
        m = mod_ref[0]
        h = _layernorm_rows(x_ref[...]) * (1.0 + m[1:2]) + m[0:1]
        h_ref[...] = h.astype(BF16)
        t = lax.broadcasted_iota(jnp.int32, mask_ref.shape[1:], 0) & (seq - 1)
        mask_ref[0] = (t != 0).astype(F32)
        mask_ref[1] = (t != seq - 1).astype(F32)

    p = _dot(h_ref[...], w_ref[...])
    rows = p.shape[0]
    wide = lambda k: jnp.concatenate([mask_ref[k]] * (p.shape[1] // LANES), axis=1)
    prev = pltpu.roll(p, 1, 0) * wide(0)
    nxt = pltpu.roll(p, rows - 1, 0) * wide(1)
    cw = cw_ref[...]
    o_ref[...] = cw[0:1] * prev + cw[1:2] * p + cw[2:3] * nxt


def _inproj(x2, mod, w_in_bf, conv_in, seq):
    tokens = x2.shape[0]
    tm = _row_tile(tokens, seq)
    nmod = mod.shape[0]
    assert nmod == 1 or tm <= seq
    tn = MXU_N
    return pl.pallas_call(
        functools.partial(_inproj_kernel, seq=seq),
        grid=(tokens // tm, C_IN // tn),
        in_specs=[pl.BlockSpec((tm, D_MODEL), lambda i, j: (i, 0)),
                  pl.BlockSpec((1, 6, D_MODEL), lambda i, j: ((i * tm // seq) % nmod, 0, 0)),
                  pl.BlockSpec((D_MODEL, tn), lambda i, j: (0, j)),
                  pl.BlockSpec((3, tn), lambda i, j: (0, j))],
        out_specs=pl.BlockSpec((tm, tn), lambda i, j: (i, j)),
        out_shape=jax.ShapeDtypeStruct((tokens, C_IN), F32),
        scratch_shapes=[pltpu.VMEM((tm, D_MODEL), BF16), pltpu.VMEM((2, tm, LANES), F32)],
        compiler_params=_cparams(("arbitrary", "arbitrary")),
        name="inproj_conv",
    )(x2, mod, w_in_bf, conv_in)


def _filter_kernel(feat_ref, win_ref, w1_ref, b1_ref, w2_ref, b2_ref, w3_ref, fr_ref,
                   fh_ref, fl_ref, a_ref, an_ref, b_ref, *, seq):
    def hp_dot(x, w):
        xh, xl = _split(x)
        wh, wl = _split(w)
        return _dot3(xh, xl, wh, wl)

    fr = fr_ref[...]
    h = jnp.sin(fr[0:1] * (hp_dot(feat_ref[...], w1_ref[...]) + b1_ref[...]))
    h = jnp.sin(fr[1:2] * (hp_dot(h, w2_ref[...]) + b2_ref[...]))
    h = hp_dot(h, w3_ref[...]) * win_ref[...]
    col = jnp.sum(jnp.abs(h), axis=0, keepdims=True)
    inv = 1.0 / (col[:, :D_HYENA] + col[:, D_HYENA:])
    hf = h[:, :D_HYENA] * inv
    hb = h[:, D_HYENA:] * inv
    row = lax.broadcasted_iota(jnp.int32, hf.shape, 0)
    hb = jnp.where(row == 0, 0.0, hb)
    eh, el = _split(hf + hb)
    oh, ol = _split(hf - hb)
    fh, fl = fh_ref[...], fl_ref[...]
    fe = _dot3(fh, fl, eh, el)
    fo = _dot3(fh[seq:], fl[seq:], oh, ol)
    a = fe[:seq]
    a_ref[...] = a
    an_ref[...] = jnp.where(row == 0, fe[seq:seq + 1], a)
    b_ref[...] = jnp.where(row == 0, 0.0, fo)


def _hyena_filter_spectrum(seq, w1, b1, w2, b2, w3, freq):
    feat, window = _filter_constants(seq)
    fh, fl, _, _ = _dft_constants(seq)
    w1p = jnp.pad(w1, ((0, FILT_HIDDEN - FILT_FEAT), (0, 0)))
    shp = jax.ShapeDtypeStruct((seq, D_HYENA), F32)
    return pl.pallas_call(
        functools.partial(_filter_kernel, seq=seq),
        out_shape=(shp, shp, shp),
        compiler_params=pltpu.CompilerParams(vmem_limit_bytes=VMEM_LIMIT),
        name="hyena_filter",
    )(feat, window, w1p, b1[None], w2, b2[None], w3, freq, fh, fl)


def _hyena_kernel(hv_ref, hx0_ref, hx1_ref, fh_ref, fl_ref, gh_ref, gl_ref, a_ref, an_ref, b_ref,
                  bias_ref, o_ref, *, seq):
    z = hx1_ref[...] * hv_ref[...]
    zb = z.astype(BF16)
    zf = _dot(fh_ref[...], zb) + _dot(fl_ref[...], zb)
    zc, zs = zf[:seq], zf[seq:]
    a, an, b = a_ref[...], an_ref[...], b_ref[...]
    yc = zc * a - zs * b
    ys = zc * b + zs * an
    yb = jnp.concatenate([yc, ys], axis=0).astype(BF16)
    y = _dot(gh_ref[...], yb) + _dot(gl_ref[...], yb)
    o_ref[...] = (hx0_ref[...] * (y + bias_ref[...] * z)).astype(o_ref.dtype)


def _hyena(p3, spec, bias, seq):
    batch = p3.shape[0]
    fh, fl, gh, gl = _dft_constants(seq)
    a, an, b = spec
    cb = MXU_N
    ncb = D_HYENA // cb
    const = lambda shape: pl.BlockSpec(shape, lambda c, i: (0, 0), pipeline_mode=pl.Buffered(1))
    col = lambda k: pl.BlockSpec((None, seq, cb), lambda c, i: (i, 0, k * ncb + c))
    chan = lambda rows: pl.BlockSpec((rows, cb), lambda c, i: (0, c))
    return pl.pallas_call(
        functools.partial(_hyena_kernel, seq=seq),
        grid=(ncb, batch),
        in_specs=[col(0), col(1), col(2),
                  const((2 * seq, seq)), const((2 * seq, seq)), const((seq, 2 * seq)), const((seq, 2 * seq)),
                  chan(seq), chan(seq), chan(seq), chan(1)],
        out_specs=pl.BlockSpec((None, seq, cb), lambda c, i: (i, 0, c)),
        out_shape=jax.ShapeDtypeStruct((batch, seq, D_HYENA), BF16),
        compiler_params=_cparams(("arbitrary", "arbitrary")),
        name="hyena_conv",
    )(p3, p3, p3, fh, fl, gh, gl, a, an, b, bias)


PAIR_TILE = 256


def _store_paired(x, xs_ref, put):
    rows = x.shape[0]
    for hp in range(HEADS // 2):
        xs_ref[hp] = x[:, hp * LANES:(hp + 1) * LANES]
    low = lax.broadcasted_iota(jnp.int32, (rows // 2, LANES), 1) < HEAD
    for hp in range(HEADS // 2):
        even = xs_ref[hp, pl.ds(0, rows // 2, stride=2), :]
        odd = xs_ref[hp, pl.ds(1, rows // 2, stride=2), :]
        put(2 * hp, jnp.where(low, even, pltpu.roll(odd, HEAD, 1)))
        put(2 * hp + 1, jnp.where(low, pltpu.roll(even, HEAD, 1), odd))


def _load_paired(get, xs_ref):
    rows = xs_ref.shape[1]
    low = lax.broadcasted_iota(jnp.int32, (rows // 2, LANES), 1) < HEAD
    for hp in range(HEADS // 2):
        b0, b1 = get(2 * hp), get(2 * hp + 1)
        xs_ref[hp, pl.ds(0, rows // 2, stride=2), :] = jnp.where(low, b0, pltpu.roll(b1, HEAD, 1))
        xs_ref[hp, pl.ds(1, rows // 2, stride=2), :] = jnp.where(low, pltpu.roll(b0, HEAD, 1), b1)
    return jnp.concatenate([xs_ref[hp] for hp in range(HEADS // 2)], axis=1)


def _rwkv_prep_kernel(r_ref, k_ref, v_ref, lora_ref, ones_ref, w0_ref, wl_ref, a0_ref, al_ref, gl_ref,
                      kk_ref, ka_ref, rk_ref,
                      sh_ref, dr_ref, bonus_ref, g_ref, xs_ref):
    r, k, v = r_ref[...], k_ref[...], v_ref[...]
    lora = lora_ref[...]
    wd = lora[:, :LORA_W]
    ad = lora[:, LORA_W:LORA_W + LORA_A]
    gd = lora[:, LORA_W + LORA_A:]
    ones = ones_ref[...]

    def shared(q):
        def put(h, val):
            sh_ref[q, h] = val
        return put

    def directed(q, d):
        def put(h, val):
            dr_ref[q, d, h] = val
        return put

    g_ref[...] = _dot(jax.nn.sigmoid(gd).astype(BF16), gl_ref[...].astype(BF16))
    kk = k * kk_ref[...]
    norm = jnp.sqrt(_dot_hi_const(kk * kk, ones))
    kk = kk / jnp.maximum(norm, NORM_EPS)
    _store_paired(r, xs_ref, shared(0))
    _store_paired(v, xs_ref, shared(1))
    _store_paired(-kk, xs_ref, shared(2))
    tw = jnp.tanh(wd).astype(BF16)
    adb = ad.astype(BF16)
    ksum = jnp.zeros_like(k)
    for d in range(2):
        w_raw = w0_ref[d:d + 1] + _dot(tw, wl_ref[d].astype(BF16))
        _store_paired(jnp.exp(-math.exp(-0.5) * jax.nn.sigmoid(w_raw)), xs_ref, directed(0, d))
        a = jax.nn.sigmoid(a0_ref[d:d + 1] + _dot(adb, al_ref[d].astype(BF16)))
        kd = k * (1.0 + (a - 1.0) * ka_ref[...])
        _store_paired(kd, xs_ref, directed(1, d))
        _store_paired(kk * a, xs_ref, directed(2, d))
        ksum = ksum + kd
    bonus_ref[...] = _dot_hi_const(r * ksum * rk_ref[...], ones) * v


def _rwkv_prep(p3, prm):
    batch, seq, _ = p3.shape
    tm = PAIR_TILE
    cbase = 3 * D_HYENA // D_RWKV
    col = lambda k: pl.BlockSpec((None, tm, D_RWKV), lambda b, i: (b, i, cbase + k))
    full = lambda a: pl.BlockSpec(a.shape, lambda b, i: (0,) * a.ndim)
    params = [prm['ones'], prm['rwkv_w0'], prm['rwkv_w_lora'], prm['rwkv_a0'], prm['rwkv_a_lora'],
              prm['rwkv_g_lora'], prm['rwkv_k_k'], prm['rwkv_k_a'], prm['rwkv_r_k']]
    shared = jax.ShapeDtypeStruct((3, batch, HEADS, seq // 2, LANES), F32)
    directed = jax.ShapeDtypeStruct((3, 2, batch, HEADS, seq // 2, LANES), F32)
    rows = jax.ShapeDtypeStruct((batch, seq, D_RWKV), F32)
    shared_spec = pl.BlockSpec((3, None, HEADS, tm // 2, LANES), lambda b, i: (0, b, 0, i, 0))
    directed_spec = pl.BlockSpec((3, 2, None, HEADS, tm // 2, LANES), lambda b, i: (0, 0, b, 0, i, 0))
    rows_spec = pl.BlockSpec((None, tm, D_RWKV), lambda b, i: (b, i, 0))
    return pl.pallas_call(
        _rwkv_prep_kernel,
        grid=(batch, seq // tm),
        in_specs=[col(0), col(1), col(2),
                  pl.BlockSpec((None, tm, LORA_ALL), lambda b, i: (b, i, (C_IN - LORA_ALL) // LORA_ALL))]
                 + [full(a) for a in params],
        out_specs=[shared_spec, directed_spec, rows_spec, rows_spec],
        out_shape=(shared, directed, rows, rows),
        scratch_shapes=[pltpu.VMEM((HEADS // 2, tm, LANES), F32)],
        compiler_params=_cparams(("arbitrary", "arbitrary")),
        name="rwkv_prep",
    )(p3, p3, p3, p3, *params)


N_SCAN_OPERANDS = 6
_R, _VV, _A, _W, _K, _B = range(N_SCAN_OPERANDS)


def _scan_kernel(*refs, mixed, m_tiles):
    if mixed:
        sf_ref, df_ref, sb_ref, db_ref, s0_ref, yf_ref, yb_ref, st_ref = refs[:8]
    else:
        sf_ref, df_ref, s0_ref, y_ref, st_ref = refs[:5]
    s_ref, in_a, in_b, yo_a, yo_b = refs[-5:]
    npair = SCAN_STEPS // 2
    tc = pl.program_id(1)
    backward = pl.program_id(0) >= m_tiles

    @pl.when(tc == 0)
    def _():
        s_ref[...] = s0_ref[...]

    low = lax.broadcasted_iota(jnp.int32, (HEAD, LANES), 1) < HEAD

    def stage(dst, jj):
        src_b = npair - 1 - jj
        src = jnp.where(backward, src_b, jj)
        for q in range(N_SCAN_OPERANDS):
            f_ref, b_ref = (sf_ref, sb_ref if mixed else None) if q < 3 else (df_ref, db_ref if mixed else None)
            qq = q % 3
            if mixed:
                t = jnp.concatenate([f_ref[qq, :, jj, :], b_ref[qq, :, src_b, :]], axis=0).T
                dst[0, q] = jnp.where(low, t[:HEAD], t[HEAD:])
                dst[1, q] = jnp.where(low, t[HEAD:], t[:HEAD])
            else:
                t = f_ref[qq, :, src, :].T
                dst[0, q] = jnp.where(backward, t[HEAD:], t[:HEAD])
                dst[1, q] = jnp.where(backward, t[:HEAD], t[HEAD:])

    def compute(src, yo):
        for sidx in range(2):
            r, w, kk = src[sidx, _R], src[sidx, _W], src[sidx, _K]
            a, b = src[sidx, _A], src[sidx, _B]
            for vb in range(HEAD // SUBLANES):
                base = vb * SUBLANES
                v8 = src[sidx, _VV, base:base + SUBLANES, :]
                ys = []
                for i in range(SUBLANES):
                    sv = s_ref[base + i]
                    sa = jnp.sum(sv * a, axis=0, keepdims=True)
                    sn = sv * w + sa * b + v8[i:i + 1, :] * kk
                    s_ref[base + i] = sn
                    ys.append(jnp.sum(sn * r, axis=0, keepdims=True))
                yo[sidx * HEAD + base:sidx * HEAD + base + SUBLANES, :] = jnp.concatenate(ys, axis=0)

    def emit(yo, jj):
        src_b = npair - 1 - jj
        z = yo[...].T
        if mixed:
            yf_ref[:, jj, :] = z[:HEAD]
            yb_ref[:, src_b, :] = pltpu.roll(z[HEAD:], HEAD, 1)
        else:
            y_ref[:, jnp.where(backward, src_b, jj), :] = jnp.where(backward, pltpu.roll(z, HEAD, 1), z)

    stage(in_a, 0)

    def two_pairs(i, carry):
        j0 = 2 * i
        stage(in_b, j0 + 1)
        compute(in_a, yo_a)
        stage(in_a, jnp.minimum(j0 + 2, npair - 1))
        emit(yo_a, j0)
        compute(in_b, yo_b)
        emit(yo_b, j0 + 1)
        return carry

    lax.fori_loop(0, npair // 2, two_pairs, 0)

    @pl.when(tc == pl.num_programs(1) - 1)
    def _():
        st_ref[...] = s_ref[...]


def _scan(shared, directed, s0):
    _, half, hseq, _ = shared.shape
    npair = SCAN_STEPS // 2
    n = hseq // npair
    mixed = half < LANES
    st = pl.BlockSpec((HEAD, HEAD, LANES), lambda l, t: (0, 0, l))
    stage_buf = pltpu.VMEM((2, N_SCAN_OPERANDS, HEAD, LANES), F32)
    y_buf = pltpu.VMEM((2 * HEAD, LANES), F32)
    scratch = [pltpu.VMEM((HEAD, HEAD, LANES), F32), stage_buf, stage_buf, y_buf, y_buf]
    st_shape = jax.ShapeDtypeStruct((HEAD, HEAD, 2 * half), F32)
    if mixed:
        assert 2 * half == LANES
        blk = (half, npair, LANES)
        sf = pl.BlockSpec((3,) + blk, lambda l, t: (0, 0, t, 0))
        sb = pl.BlockSpec((3,) + blk, lambda l, t: (0, 0, n - 1 - t, 0))
        df = pl.BlockSpec((3, None) + blk, lambda l, t: (0, 0, 0, t, 0))
        db = pl.BlockSpec((3, None) + blk, lambda l, t: (0, 1, 0, n - 1 - t, 0))
        yf_spec = pl.BlockSpec(blk, lambda l, t: (0, t, 0))
        yb_spec = pl.BlockSpec(blk, lambda l, t: (0, n - 1 - t, 0))
        yshape = jax.ShapeDtypeStruct((half, hseq, LANES), F32)
        yf, yb, stt = pl.pallas_call(
            functools.partial(_scan_kernel, mixed=True, m_tiles=1),
            grid=(1, n),
            in_specs=[sf, df, sb, db, st],
            out_specs=[yf_spec, yb_spec, st],
            out_shape=(yshape, yshape, st_shape),
            scratch_shapes=scratch,
            compiler_params=_cparams(("arbitrary", "arbitrary")),
            name="rwkv_scan_mixed",
        )(shared, directed, shared, directed, s0)
        return yf, yb, 0, 0, stt
    assert half % LANES == 0
    m = half // LANES
    chunk = lambda l, t: jnp.where(l >= m, n - 1 - t, t)
    blk = (LANES, npair, LANES)
    sp = pl.BlockSpec((3,) + blk, lambda l, t: (0, l % m, chunk(l, t), 0))
    dp = pl.BlockSpec((3, None) + blk, lambda l, t: (0, l // m, l % m, chunk(l, t), 0))
    yp = pl.BlockSpec((None,) + blk, lambda l, t: (l // m, l % m, chunk(l, t), 0))
    y, stt = pl.pallas_call(
        functools.partial(_scan_kernel, mixed=False, m_tiles=m),
        grid=(2 * m, n),
        in_specs=[sp, dp, st],
        out_specs=[yp, st],
        out_shape=(jax.ShapeDtypeStruct((2, half, hseq, LANES), F32), st_shape),
        scratch_shapes=scratch,
        compiler_params=_cparams(("arbitrary", "arbitrary")),
        name="rwkv_scan",
    )(shared, directed, s0)
    y2 = y.reshape(2 * half, hseq, LANES)
    return y2, y2, 0, half, stt


def _mix_kernel(yf_ref, yb_ref, bonus_ref, g_ref, yh_ref, x_ref, mod_ref, ones_ref, wo_ref, lng_ref, lnb_ref,
                l1g_ref, l1b_ref, o_ref, xs_ref):
    ones = ones_ref[...]
    y = _load_paired(lambda h: yf_ref[h] + yb_ref[h], xs_ref)
    mu = _dot_hi_const(y, ones) * (1.0 / HEAD)
    yc = y - mu
    var = _dot_hi_const(yc * yc, ones) * (1.0 / HEAD)
    yr = yc * lax.rsqrt(var + GN_EPS) * lng_ref[...] + lnb_ref[...] + bonus_ref[...]
    yr = (yr * g_ref[...]).astype(BF16)
    mix = _dot(yh_ref[...], wo_ref[:D_HYENA, :]) + _dot(yr, wo_ref[D_HYENA:, :])
    m = mod_ref[0]
    z = _layernorm_rows(ALPHA * x_ref[...] + m[2:3] * mix)
    o_ref[...] = z * l1g_ref[...] + l1b_ref[...]


def _mix(yf, yb, off_f, off_b, bonus, g, yh, x, mod, prm):
    batch, seq, _ = x.shape
    tm = PAIR_TILE
    nmod = mod.shape[0]
    yf4 = yf.reshape(-1, HEADS, seq // 2, LANES)
    yb4 = yb.reshape(-1, HEADS, seq // 2, LANES)
    bf, bb = off_f // HEADS, off_b // HEADS
    row = lambda width: pl.BlockSpec((None, tm, width), lambda b, i: (b, i, 0))
    full = lambda a: pl.BlockSpec(a.shape, lambda b, i: (0,) * a.ndim)
    params = [prm['ones'], prm['w_out'], prm['lnx_g'], prm['lnx_b'], prm['ln1_g'], prm['ln1_b']]
    return pl.pallas_call(
        _mix_kernel,
        grid=(batch, seq // tm),
        in_specs=[pl.BlockSpec((None, HEADS, tm // 2, LANES), lambda b, i: (bf + b, 0, i, 0)),
                  pl.BlockSpec((None, HEADS, tm // 2, LANES), lambda b, i: (bb + b, 0, i, 0)),
                  row(D_RWKV), row(D_RWKV), row(D_HYENA), row(D_MODEL),
                  pl.BlockSpec((1, 6, D_MODEL), lambda b, i: (b % nmod, 0, 0))]
                 + [full(a) for a in params],
        out_specs=row(D_MODEL),
        out_shape=jax.ShapeDtypeStruct((batch, seq, D_MODEL), F32),
        scratch_shapes=[pltpu.VMEM((HEADS // 2, tm, LANES), F32)],
        compiler_params=_cparams(("arbitrary", "arbitrary")),
        name="mix_outproj",
    )(yf4, yb4, bonus, g, yh, x, mod, *params)


def _ffn_kernel(x_ref, mod_ref, wu_ref, wg_ref, cw_ref, cb_ref, wd_ref, l2g_ref, l2b_ref, o_ref,
                h_ref, acc_ref, mask_ref, *, seq, on_grid):
    j = pl.program_id(1)
    width = GRID_W if on_grid else seq

    @pl.when(j == 0)
    def _():
        m = mod_ref[0]
        h = _layernorm_rows(x_ref[...]) * (1.0 + m[4:5]) + m[3:4]
        h_ref[...] = h.astype(BF16)
        acc_ref[...] = jnp.zeros_like(acc_ref)
        t = lax.broadcasted_iota(jnp.int32, mask_ref.shape[1:], 0)
        col = t & (width - 1)
        grow = (t & (seq - 1)) >> int(math.log2(GRID_W))
        mask_ref[0] = (col != 0).astype(F32)
        mask_ref[1] = (col != width - 1).astype(F32)
        mask_ref[2] = (grow != 0).astype(F32)
        mask_ref[3] = (grow != seq // GRID_W - 1).astype(F32)

    h = h_ref[...]
    u = _dot(h, wu_ref[...])
    gt = _dot(h, wg_ref[...])
    rows = u.shape[0]
    cw = cw_ref[...]
    wide = lambda k: jnp.concatenate([mask_ref[k]] * (u.shape[1] // LANES), axis=1)
    um = pltpu.roll(u, 1, 0) * wide(0)
    up = pltpu.roll(u, rows - 1, 0) * wide(1)
    if on_grid:
        hrow = [cw[3 * di:3 * di + 1] * um + cw[3 * di + 1:3 * di + 2] * u + cw[3 * di + 2:3 * di + 3] * up
                for di in range(3)]
        u = (hrow[1] + pltpu.roll(hrow[0], GRID_W, 0) * wide(2)
             + pltpu.roll(hrow[2], rows - GRID_W, 0) * wide(3))
    else:
        u = cw[3:4] * um + cw[4:5] * u + cw[5:6] * up
    u = u + cb_ref[...]
    act = 0.5 * u * (1.0 + jnp.tanh(math.sqrt(2.0 / math.pi) * (u + 0.044715 * (u * u * u))))
    acc_ref[...] += _dot((act * gt).astype(BF16), wd_ref[...])

    @pl.when(j == pl.num_programs(1) - 1)
    def _():
        m = mod_ref[0]
        z = _layernorm_rows(ALPHA * x_ref[...] + m[5:6] * acc_ref[...])
        o_ref[...] = z * l2g_ref[...] + l2b_ref[...]


def _ffn(x2, mod, seq, on_grid, prm):
    tokens = x2.shape[0]
    tm = _row_tile(tokens, seq)
    nmod = mod.shape[0]
    assert nmod == 1 or tm <= seq
    assert not on_grid or tm % seq == 0
    tf = MXU_N
    nf = D_FF // tf
    full = lambda a: pl.BlockSpec(a.shape, lambda i, j: (0,) * a.ndim)
    return pl.pallas_call(
        functools.partial(_ffn_kernel, seq=seq, on_grid=on_grid),
        grid=(tokens // tm, nf),
        in_specs=[pl.BlockSpec((tm, D_MODEL), lambda i, j: (i, 0)),
                  pl.BlockSpec((1, 6, D_MODEL), lambda i, j: ((i * tm // seq) % nmod, 0, 0)),
                  pl.BlockSpec((D_MODEL, tf), lambda i, j: (0, j)),
                  pl.BlockSpec((D_MODEL, tf), lambda i, j: (0, nf + j)),
                  pl.BlockSpec((9, tf), lambda i, j: (0, j)),
                  pl.BlockSpec((1, tf), lambda i, j: (0, j)),
                  pl.BlockSpec((tf, D_MODEL), lambda i, j: (j, 0)),
                  full(prm['ln2_g']), full(prm['ln2_b'])],
        out_specs=pl.BlockSpec((tm, D_MODEL), lambda i, j: (i, 0)),
        out_shape=jax.ShapeDtypeStruct((tokens, D_MODEL), F32),
        scratch_shapes=[pltpu.VMEM((tm, D_MODEL), BF16), pltpu.VMEM((tm, D_MODEL), F32),
                        pltpu.VMEM((4, tm, LANES), F32)],
        compiler_params=_cparams(("arbitrary", "arbitrary")),
        name="conv_ffn",
    )(x2, mod, prm['w_ffn_up'], prm['w_ffn_up'], prm['ffn_conv_w'], prm['ffn_conv_b'], prm['w_ffn_down'],
      prm['ln2_g'], prm['ln2_b'])


def _layer(x, mod, s0, on_grid, prm):
    batch, seq, _ = x.shape
    tokens = batch * seq
    p2 = _inproj(x.reshape(tokens, D_MODEL), mod, prm['w_in'], prm['conv_in'], seq)
    p3 = p2.reshape(batch, seq, C_IN)

    spec = _hyena_filter_spectrum(seq, prm['filt_w1'], prm['filt_b1'], prm['filt_w2'], prm['filt_b2'],
                                  prm['filt_w3'], prm['filt_freq'])
    yh = _hyena(p3, spec, prm['hyena_bias'], seq)

    shared, directed, bonus, g = _rwkv_prep(p3, prm)
    half = batch * HEADS
    flat = lambda a: a.reshape(a.shape[:-4] + (half, seq // 2, LANES))
    if s0 is None:
        s0l = jnp.zeros((HEAD, HEAD, 2 * half), F32)
    else:
        s0l = jnp.transpose(s0, (3, 4, 1, 0, 2)).reshape(HEAD, HEAD, 2 * half)
    yf, yb, off_f, off_b, st = _scan(flat(shared), flat(directed), s0l)
    s_final = jnp.transpose(st.reshape(HEAD, HEAD, 2, batch, HEADS), (3, 2, 4, 0, 1))

    x1 = _mix(yf, yb, off_f, off_b, bonus, g, yh, x, mod, prm)
    out = _ffn(x1.reshape(tokens, D_MODEL), mod, seq, on_grid, prm)
    return out.reshape(batch, seq, D_MODEL), s_final


def kernel(x_prompt, x_sample, state_rwkv, c, c_ctx, ada_w, ada_b, w_in, conv_in, filt_w1, filt_b1, filt_w2, filt_b2, filt_w3, filt_freq, hyena_bias, rwkv_w0, rwkv_w_lora, rwkv_a0, rwkv_a_lora, rwkv_g_lora, rwkv_k_k, rwkv_k_a, rwkv_r_k, lnx_g, lnx_b, w_out, ln1_g, ln1_b, w_ffn_up, ffn_conv_w, ffn_conv_b, w_ffn_down, ln2_g, ln2_b):
    dec_batch = x_sample.shape[0]
    ones = jnp.asarray(_head_ones())
    y_prompt, y_sample = x_prompt, x_sample
    ctx_states = []
    for layer in range(DEPTH):
        row = lambda a: a[layer][None]
        prm = dict(
            ones=ones, w_in=w_in[layer].astype(BF16), conv_in=conv_in[layer],
            filt_w1=filt_w1[layer], filt_b1=filt_b1[layer], filt_w2=filt_w2[layer], filt_b2=filt_b2[layer],
            filt_w3=filt_w3[layer], filt_freq=filt_freq[layer], hyena_bias=row(hyena_bias),
            rwkv_w0=rwkv_w0[layer], rwkv_w_lora=rwkv_w_lora[layer], rwkv_a0=rwkv_a0[layer],
            rwkv_a_lora=rwkv_a_lora[layer], rwkv_g_lora=rwkv_g_lora[layer], rwkv_k_k=row(rwkv_k_k),
            rwkv_k_a=row(rwkv_k_a), rwkv_r_k=rwkv_r_k[layer].reshape(1, D_RWKV),
            lnx_g=row(lnx_g), lnx_b=row(lnx_b), w_out=w_out[layer].astype(BF16),
            ln1_g=row(ln1_g), ln1_b=row(ln1_b), w_ffn_up=w_ffn_up[layer].astype(BF16),
            ffn_conv_w=ffn_conv_w[layer].reshape(9, D_FF), ffn_conv_b=row(ffn_conv_b),
            w_ffn_down=w_ffn_down[layer].astype(BF16), ln2_g=row(ln2_g), ln2_b=row(ln2_b))
        pad = (-(dec_batch + 1)) % SUBLANES
        cond = jnp.concatenate([c, c_ctx[None], jnp.zeros((pad, D_MODEL), F32)], axis=0)
        mod = _ada(cond, ada_w[layer], row(ada_b)).reshape(cond.shape[0], 6, D_MODEL)
        y_prompt, s_ctx = _layer(y_prompt, mod[dec_batch:dec_batch + 1], None, False, prm)
        ctx_states.append(s_ctx)
        y_sample, _ = _layer(y_sample, mod[:dec_batch], state_rwkv[:, layer], True, prm)
    return (y_prompt, y_sample, jnp.stack(ctx_states, axis=1))
```

```python
import functools
import math

import ml_dtypes
import numpy as np
import jax
import jax.numpy as jnp
from jax import lax
from jax.experimental import pallas as pl
from jax.experimental.pallas import tpu as pltpu

F32 = jnp.float32
BF16 = jnp.bfloat16

D_MODEL = 1024
D_HYENA = 512
D_RWKV = 512
HEAD = 64
HEADS = D_RWKV // HEAD
LORA_W = 64
LORA_A = 64
LORA_G = 128
LORA_ALL = LORA_W + LORA_A + LORA_G
C_IN = 3 * D_HYENA + 3 * D_RWKV + LORA_ALL
FILT_BANDS = 16
FILT_FEAT = 1 + 2 * FILT_BANDS
FILT_HIDDEN = 64
N_FILT = 2 * D_HYENA
HYENA_TARGET = 1e-2
HYENA_FAST_PCT = 0.3
HYENA_SLOW_PCT = 1.5
D_FF = 2816
GRID_W = 64
DEPTH = 1
ALPHA = (2.0 * DEPTH) ** 0.25
LN_EPS = 1e-5
GN_EPS = 64e-5
NORM_EPS = 1e-12

LANES = 128
SUBLANES = 8
MXU_N = 256
ROW_TILE = 1024
SCAN_STEPS = 32
VMEM_LIMIT = 56 * 1024 * 1024


def _cparams(sem):
    return pltpu.CompilerParams(dimension_semantics=sem, vmem_limit_bytes=VMEM_LIMIT)


def _dot(a, b):
    return jnp.dot(a, b, preferred_element_type=F32)


def _split(x):
    hi = x.astype(BF16)
    lo = (x - hi.astype(F32)).astype(BF16)
    return hi, lo


def _dot_hi_const(x, c):
    hi, lo = _split(x)
    return _dot(hi, c) + _dot(lo, c)


def _dot3(ah, al, bh, bl):
    return _dot(ah, bh) + _dot(ah, bl) + _dot(al, bh)


def _layernorm_rows(x):
    mu = jnp.mean(x, axis=-1, keepdims=True)
    xc = x - mu
    var = jnp.mean(xc * xc, axis=-1, keepdims=True)
    return xc * lax.rsqrt(var + LN_EPS)


def _row_tile(tokens, seq):
    for tm in (ROW_TILE, 512, 256):
        if tokens % tm == 0 and (tm % seq == 0 or seq % tm == 0):
            return tm
    raise ValueError(f"no row tile for {tokens} tokens of sequence length {seq}")


def _np_split(x64):
    hi = x64.astype(ml_dtypes.bfloat16)
    lo = (x64 - hi.astype(np.float64)).astype(ml_dtypes.bfloat16)
    return hi, lo


@functools.lru_cache(maxsize=None)
def _dft_constants(L):
    N = 2 * L
    f = np.arange(L, dtype=np.int64)[:, None]
    t = np.arange(L, dtype=np.int64)[None, :]
    ang = (2.0 * np.pi / N) * ((f * t) % N).astype(np.float64)
    cos, sin = np.cos(ang), np.sin(ang)
    nyq = np.where(np.arange(L) % 2 == 0, 1.0, -1.0)
    fwd_s = sin.copy()
    fwd_s[0, :] = nyq
    fwd = np.concatenate([cos, fwd_s], axis=0)
    wf = np.full((L, 1), 2.0)
    wf[0, 0] = 1.0
    inv_c = (wf * cos).T / N
    inv_s = (2.0 * sin).T / N
    inv_s[:, 0] = nyq / N
    inv = np.concatenate([inv_c, inv_s], axis=1)
    return _np_split(fwd) + _np_split(inv)


@functools.lru_cache(maxsize=None)
def _filter_constants(L):
    t = np.arange(L, dtype=np.float64)[:, None] / L
    bands = np.arange(1, FILT_BANDS + 1, dtype=np.float64)[None, :]
    ang = (2.0 * math.pi) * bands * t
    feat = np.concatenate([t, np.sin(ang), np.cos(ang)], axis=-1)
    feat = np.pad(feat, ((0, 0), (0, FILT_HIDDEN - FILT_FEAT)))
    slow = abs(math.log(HYENA_TARGET) / HYENA_SLOW_PCT)
    fast = abs(math.log(HYENA_TARGET) / HYENA_FAST_PCT)
    deltas = np.linspace(slow, fast, N_FILT, dtype=np.float64)
    window = np.exp(-t * deltas[None, :])
    return feat.astype(np.float32), window.astype(np.float32)


@functools.lru_cache(maxsize=None)
def _head_ones():
    h = np.arange(D_RWKV) // HEAD
    return (h[:, None] == h[None, :]).astype(ml_dtypes.bfloat16)


@functools.lru_cache(maxsize=None)
def _chunk_sum_matrices(rows):
    t = np.arange(rows)
    same = (t[:, None] // SCAN_STEPS) == (t[None, :] // SCAN_STEPS)
    fwd = same & (t[None, :] <= t[:, None])
    bwd = same & (t[None, :] >= t[:, None])
    return np.stack([fwd, bwd]).astype(ml_dtypes.bfloat16)


def _ada_kernel(c_ref, w_ref, b_ref, o_ref):
    c = c_ref[...]
    s = c * jax.nn.sigmoid(c)
    o_ref[...] = _dot(s.astype(BF16), w_ref[...].astype(BF16)) + b_ref[...]


def _ada(cond, ada_w, ada_b):
    rows = cond.shape[0]
    tn = 1024
    return pl.pallas_call(
        _ada_kernel,
        grid=(6 * D_MODEL // tn,),
        in_specs=[pl.BlockSpec((rows, D_MODEL), lambda j: (0, 0)),
                  pl.BlockSpec((D_MODEL, tn), lambda j: (0, j)),
                  pl.BlockSpec((1, tn), lambda j: (0, j))],
        out_specs=pl.BlockSpec((rows, tn), lambda j: (0, j)),
        out_shape=jax.ShapeDtypeStruct((rows, 6 * D_MODEL), F32),
        compiler_params=_cparams(("arbitrary",)),
        name="ada_mod",
    )(cond, ada_w, ada_b)


def _inproj_kernel(x_ref, mod_ref, w_ref, cw_ref, o_ref, h_ref, mask_ref, *, seq):
    @pl.when(pl.program_id(1) == 0)
    def _():
        m = mod_ref[0]
        h = _layernorm_rows(x_ref[...]) * (1.0 + m[1:2]) + m[0:1]
        h_ref[...] = h.astype(BF16)
        t = lax.broadcasted_iota(jnp.int32, mask_ref.shape[1:], 0) & (seq - 1)
        mask_ref[0] = (t != 0).astype(F32)
        mask_ref[1] = (t != seq - 1).astype(F32)

    p = _dot(h_ref[...], w_ref[...])
    rows = p.shape[0]
    wide = lambda k: jnp.concatenate([mask_ref[k]] * (p.shape[1] // LANES), axis=1)
    prev = pltpu.roll(p, 1, 0) * wide(0)
    nxt = pltpu.roll(p, rows - 1, 0) * wide(1)
    cw = cw_ref[...]
    o_ref[...] = cw[0:1] * prev + cw[1:2] * p + cw[2:3] * nxt


def _inproj(x2, mod, w_in_bf, conv_in, seq):
    tokens = x2.shape[0]
    tm = _row_tile(tokens, seq)
    nmod = mod.shape[0]
    assert nmod == 1 or tm <= seq
    tn = MXU_N
    return pl.pallas_call(
        functools.partial(_inproj_kernel, seq=seq),
        grid=(tokens // tm, C_IN // tn),
        in_specs=[pl.BlockSpec((tm, D_MODEL), lambda i, j: (i, 0)),
                  pl.BlockSpec((1, 6, D_MODEL), lambda i, j: ((i * tm // seq) % nmod, 0, 0)),
                  pl.BlockSpec((D_MODEL, tn), lambda i, j: (0, j)),
                  pl.BlockSpec((3, tn), lambda i, j: (0, j))],
        out_specs=pl.BlockSpec((tm, tn), lambda i, j: (i, j)),
        out_shape=jax.ShapeDtypeStruct((tokens, C_IN), F32),
        scratch_shapes=[pltpu.VMEM((tm, D_MODEL), BF16), pltpu.VMEM((2, tm, LANES), F32)],
        compiler_params=_cparams(("arbitrary", "arbitrary")),
        name="inproj_conv",
    )(x2, mod, w_in_bf, conv_in)


def _filter_kernel(feat_ref, win_ref, w1_ref, b1_ref, w2_ref, b2_ref, w3_ref, fr_ref,
                   fh_ref, fl_ref, a_ref, an_ref, b_ref, *, seq):
    def hp_dot(x, w):
        xh, xl = _split(x)
        wh, wl = _split(w)
        return _dot3(xh, xl, wh, wl)

    fr = fr_ref[...]
    h = jnp.sin(fr[0:1] * (hp_dot(feat_ref[...], w1_ref[...]) + b1_ref[...]))
    h = jnp.sin(fr[1:2] * (hp_dot(h, w2_ref[...]) + b2_ref[...]))
    h = hp_dot(h, w3_ref[...]) * win_ref[...]
    col = jnp.sum(jnp.abs(h), axis=0, keepdims=True)
    inv = 1.0 / (col[:, :D_HYENA] + col[:, D_HYENA:])
    hf = h[:, :D_HYENA] * inv
    hb = h[:, D_HYENA:] * inv
    row = lax.broadcasted_iota(jnp.int32, hf.shape, 0)
    hb = jnp.where(row == 0, 0.0, hb)
    eh, el = _split(hf + hb)
    oh, ol = _split(hf - hb)
    fh, fl = fh_ref[...], fl_ref[...]
    fe = _dot3(fh, fl, eh, el)
    fo = _dot3(fh[seq:], fl[seq:], oh, ol)
    a = fe[:seq]
    a_ref[...] = a
    an_ref[...] = jnp.where(row == 0, fe[seq:seq + 1], a)
    b_ref[...] = jnp.where(row == 0, 0.0, fo)


def _hyena_filter_spectrum(seq, w1, b1, w2, b2, w3, freq):
    feat, window = _filter_constants(seq)
    fh, fl, _, _ = _dft_constants(seq)
    w1p = jnp.pad(w1, ((0, FILT_HIDDEN - FILT_FEAT), (0, 0)))
    shp = jax.ShapeDtypeStruct((seq, D_HYENA), F32)
    return pl.pallas_call(
        functools.partial(_filter_kernel, seq=seq),
        out_shape=(shp, shp, shp),
        compiler_params=pltpu.CompilerParams(vmem_limit_bytes=VMEM_LIMIT),
        name="hyena_filter",
    )(feat, window, w1p, b1[None], w2, b2[None], w3, freq, fh, fl)


def _hyena_kernel(hv_ref, hx0_ref, hx1_ref, fh_ref, fl_ref, gh_ref, gl_ref, a_ref, an_ref, b_ref,
                  bias_ref, o_ref, *, seq):
    z = hx1_ref[...] * hv_ref[...]
    zb = z.astype(BF16)
    zf = _dot(fh_ref[...], zb) + _dot(fl_ref[...], zb)
    zc, zs = zf[:seq], zf[seq:]
    a, an, b = a_ref[...], an_ref[...], b_ref[...]
    yc = zc * a - zs * b
    ys = zc * b + zs * an
    yb = jnp.concatenate([yc, ys], axis=0).astype(BF16)
    y = _dot(gh_ref[...], yb) + _dot(gl_ref[...], yb)
    o_ref[...] = (hx0_ref[...] * (y + bias_ref[...] * z)).astype(o_ref.dtype)


def _hyena(p3, spec, bias, seq):
    batch = p3.shape[0]
    fh, fl, gh, gl = _dft_constants(seq)
    a, an, b = spec
    cb = MXU_N
    ncb = D_HYENA // cb
    const = lambda shape: pl.BlockSpec(shape, lambda c, i: (0, 0), pipeline_mode=pl.Buffered(1))
    col = lambda k: pl.BlockSpec((None, seq, cb), lambda c, i: (i, 0, k * ncb + c))
    chan = lambda rows: pl.BlockSpec((rows, cb), lambda c, i: (0, c))
    return pl.pallas_call(
        functools.partial(_hyena_kernel, seq=seq),
        grid=(ncb, batch),
        in_specs=[col(0), col(1), col(2),
                  const((2 * seq, seq)), const((2 * seq, seq)), const((seq, 2 * seq)), const((seq, 2 * seq)),
                  chan(seq), chan(seq), chan(seq), chan(1)],
        out_specs=pl.BlockSpec((None, seq, cb), lambda c, i: (i, 0, c)),
        out_shape=jax.ShapeDtypeStruct((batch, seq, D_HYENA), BF16),
        compiler_params=_cparams(("arbitrary", "arbitrary")),
        name="hyena_conv",
    )(p3, p3, p3, fh, fl, gh, gl, a, an, b, bias)


PAIR_TILE = 256


def _store_paired(x, xs_ref, put):
    rows = x.shape[0]
    for hp in range(HEADS // 2):
        xs_ref[hp] = x[:, hp * LANES:(hp + 1) * LANES]
    low = lax.broadcasted_iota(jnp.int32, (rows // 2, LANES), 1) < HEAD
    for hp in range(HEADS // 2):
        even = xs_ref[hp, pl.ds(0, rows // 2, stride=2), :]
        odd = xs_ref[hp, pl.ds(1, rows // 2, stride=2), :]
        put(2 * hp, jnp.where(low, even, pltpu.roll(odd, HEAD, 1)))
        put(2 * hp + 1, jnp.where(low, pltpu.roll(even, HEAD, 1), odd))


def _load_paired(get, xs_ref):
    rows = xs_ref.shape[1]
    low = lax.broadcasted_iota(jnp.int32, (rows // 2, LANES), 1) < HEAD
    for hp in range(HEADS // 2):
        b0, b1 = get(2 * hp), get(2 * hp + 1)
        xs_ref[hp, pl.ds(0, rows // 2, stride=2), :] = jnp.where(low, b0, pltpu.roll(b1, HEAD, 1))
        xs_ref[hp, pl.ds(1, rows // 2, stride=2), :] = jnp.where(low, pltpu.roll(b0, HEAD, 1), b1)
    return jnp.concatenate([xs_ref[hp] for hp in range(HEADS // 2)], axis=1)


def _rwkv_prep_kernel(r_ref, k_ref, v_ref, lora_ref, ones_ref, w0_ref, wl_ref, a0_ref, al_ref, gl_ref,
                      kk_ref, ka_ref, rk_ref,
                      tri_ref, sh_ref, dr_ref, bonus_ref, g_ref, rt_ref, bt_ref, kt_ref, xs_ref):
    r, k, v = r_ref[...], k_ref[...], v_ref[...]
    lora = lora_ref[...]
    wd = lora[:, :LORA_W]
    ad = lora[:, LORA_W:LORA_W + LORA_A]
    gd = lora[:, LORA_W + LORA_A:]
    ones = ones_ref[...]

    def shared(q):
        def put(h, val):
            sh_ref[q, h] = val
        return put

    def directed(q, d):
        def put(h, val):
            dr_ref[q, d, h] = val
        return put

    g_ref[...] = _dot(jax.nn.sigmoid(gd).astype(BF16), gl_ref[...].astype(BF16))
    kk = k * kk_ref[...]
    norm = jnp.sqrt(_dot_hi_const(kk * kk, ones))
    kk = kk / jnp.maximum(norm, NORM_EPS)
    _store_paired(v, xs_ref, shared(0))
    _store_paired(-kk, xs_ref, shared(1))
    tw = jnp.tanh(wd).astype(BF16)
    adb = ad.astype(BF16)
    ksum = jnp.zeros_like(k)
    for d in range(2):
        w_raw = w0_ref[d:d + 1] + _dot(tw, wl_ref[d].astype(BF16))
        lw = -math.exp(-0.5) * jax.nn.sigmoid(w_raw)
        _store_paired(jnp.exp(lw), xs_ref, directed(0, d))
        a = jax.nn.sigmoid(a0_ref[d:d + 1] + _dot(adb, al_ref[d].astype(BF16)))
        kd = k * (1.0 + (a - 1.0) * ka_ref[...])
        _store_paired(kd, xs_ref, directed(1, d))
        bd = kk * a
        _store_paired(bd, xs_ref, directed(2, d))
        ksum = ksum + kd
        l1 = lw.astype(BF16)
        l2 = (lw - l1.astype(F32)).astype(BF16)
        l3 = (lw - l1.astype(F32) - l2.astype(F32)).astype(BF16)
        tri = tri_ref[d]
        cs = _dot(tri, l1) + _dot(tri, l2) + _dot(tri, l3)
        grow, shrink = jnp.exp(-cs), jnp.exp(cs)
        rt_ref[d] = (r * shrink).astype(BF16)
        bt_ref[d] = (bd * grow).astype(BF16)
        kt_ref[d] = (kd * grow).astype(BF16)
    bonus_ref[...] = _dot_hi_const(r * ksum * rk_ref[...], ones) * v


def _rwkv_prep(p3, prm):
    batch, seq, _ = p3.shape
    tm = PAIR_TILE
    cbase = 3 * D_HYENA // D_RWKV
    col = lambda k: pl.BlockSpec((None, tm, D_RWKV), lambda b, i: (b, i, cbase + k))
    full = lambda a: pl.BlockSpec(a.shape, lambda b, i: (0,) * a.ndim)
    params = [prm['ones'], prm['rwkv_w0'], prm['rwkv_w_lora'], prm['rwkv_a0'], prm['rwkv_a_lora'],
              prm['rwkv_g_lora'], prm['rwkv_k_k'], prm['rwkv_k_a'], prm['rwkv_r_k'],
              jnp.asarray(_chunk_sum_matrices(tm))]
    shared = jax.ShapeDtypeStruct((2, batch, HEADS, seq // 2, LANES), F32)
    directed = jax.ShapeDtypeStruct((3, 2, batch, HEADS, seq // 2, LANES), F32)
    rows = jax.ShapeDtypeStruct((batch, seq, D_RWKV), F32)
    shared_spec = pl.BlockSpec((2, None, HEADS, tm // 2, LANES), lambda b, i: (0, b, 0, i, 0))
    scaled = jax.ShapeDtypeStruct((2, batch, seq, D_RWKV), BF16)
    scaled_spec = pl.BlockSpec((2, None, tm, D_RWKV), lambda b, i: (0, b, i, 0))
    directed_spec = pl.BlockSpec((3, 2, None, HEADS, tm // 2, LANES), lambda b, i: (0, 0, b, 0, i, 0))
    rows_spec = pl.BlockSpec((None, tm, D_RWKV), lambda b, i: (b, i, 0))
    return pl.pallas_call(
        _rwkv_prep_kernel,
        grid=(batch, seq // tm),
        in_specs=[col(0), col(1), col(2),
                  pl.BlockSpec((None, tm, LORA_ALL), lambda b, i: (b, i, (C_IN - LORA_ALL) // LORA_ALL))]
                 + [full(a) for a in params],
        out_specs=[shared_spec, directed_spec, rows_spec, rows_spec, scaled_spec, scaled_spec, scaled_spec],
        out_shape=(shared, directed, rows, rows, scaled, scaled, scaled),
        scratch_shapes=[pltpu.VMEM((HEADS // 2, tm, LANES), F32)],
        compiler_params=_cparams(("arbitrary", "arbitrary")),
        name="rwkv_prep",
    )(p3, p3, p3, p3, *params)


N_SHARED = 2
N_SCAN_OPERANDS = 5
_VV, _A, _W, _K, _B = range(N_SCAN_OPERANDS)


def _scan_kernel(*refs, mixed, m_tiles):
    if mixed:
        sf_ref, df_ref, sb_ref, db_ref, s0_ref, yf_ref, yb_ref, pf_ref, pb_ref, st_ref = refs[:10]
    else:
        sf_ref, df_ref, s0_ref, y_ref, p_ref, st_ref = refs[:6]
    s_ref, in_a, in_b, yo_a, yo_b = refs[-5:]
    npair = SCAN_STEPS // 2
    tc = pl.program_id(1)
    backward = pl.program_id(0) >= m_tiles

    @pl.when(tc == 0)
    def _():
        s_ref[...] = s0_ref[...]

    low = lax.broadcasted_iota(jnp.int32, (HEAD, LANES), 1) < HEAD

    def snapshot(j, carry):
        z = jnp.concatenate([s_ref[2 * j], s_ref[2 * j + 1]], axis=0).T
        if mixed:
            pf_ref[:, j, :] = z[:HEAD]
            pb_ref[:, j, :] = z[HEAD:]
        else:
            p_ref[:, j, :] = z
        return carry

    lax.fori_loop(0, HEAD // 2, snapshot, 0)

    def stage(dst, jj):
        src_b = npair - 1 - jj
        src = jnp.where(backward, src_b, jj)
        for q in range(N_SCAN_OPERANDS):
            if q < N_SHARED:
                f_ref, b_ref, qq = sf_ref, (sb_ref if mixed else None), q
            else:
                f_ref, b_ref, qq = df_ref, (db_ref if mixed else None), q - N_SHARED
            if mixed:
                t = jnp.concatenate([f_ref[qq, :, jj, :], b_ref[qq, :, src_b, :]], axis=0).T
                dst[0, q] = jnp.where(low, t[:HEAD], t[HEAD:])
                dst[1, q] = jnp.where(low, t[HEAD:], t[:HEAD])
            else:
                t = f_ref[qq, :, src, :].T
                dst[0, q] = jnp.where(backward, t[HEAD:], t[:HEAD])
                dst[1, q] = jnp.where(backward, t[:HEAD], t[HEAD:])

    def compute(src, yo):
        for sidx in range(2):
            w, kk = src[sidx, _W], src[sidx, _K]
            a, b = src[sidx, _A], src[sidx, _B]
            for vb in range(HEAD // SUBLANES):
                base = vb * SUBLANES
                v8 = src[sidx, _VV, base:base + SUBLANES, :]
                sas = []
                for i in range(SUBLANES):
                    sv = s_ref[base + i]
                    sa = jnp.sum(sv * a, axis=0, keepdims=True)
                    s_ref[base + i] = sv * w + sa * b + v8[i:i + 1, :] * kk
                    sas.append(sa)
                yo[sidx * HEAD + base:sidx * HEAD + base + SUBLANES, :] = jnp.concatenate(sas, axis=0)

    def emit(yo, jj):
        src_b = npair - 1 - jj
        z = yo[...].T
        if mixed:
            yf_ref[:, jj, :] = z[:HEAD]
            yb_ref[:, src_b, :] = pltpu.roll(z[HEAD:], HEAD, 1)
        else:
            y_ref[:, jnp.where(backward, src_b, jj), :] = jnp.where(backward, pltpu.roll(z, HEAD, 1), z)

    stage(in_a, 0)

    def two_pairs(i, carry):
        j0 = 2 * i
        stage(in_b, j0 + 1)
        compute(in_a, yo_a)
        stage(in_a, jnp.minimum(j0 + 2, npair - 1))
        emit(yo_a, j0)
        compute(in_b, yo_b)
        emit(yo_b, j0 + 1)
        return carry

    lax.fori_loop(0, npair // 2, two_pairs, 0)

    @pl.when(tc == pl.num_programs(1) - 1)
    def _():
        st_ref[...] = s_ref[...]


def _scan(shared, directed, s0):
    _, half, hseq, _ = shared.shape
    npair = SCAN_STEPS // 2
    n = hseq // npair
    mixed = half < LANES
    st = pl.BlockSpec((HEAD, HEAD, LANES), lambda l, t: (0, 0, l))
    stage_buf = pltpu.VMEM((2, N_SCAN_OPERANDS, HEAD, LANES), F32)
    y_buf = pltpu.VMEM((2 * HEAD, LANES), F32)
    scratch = [pltpu.VMEM((HEAD, HEAD, LANES), F32), stage_buf, stage_buf, y_buf, y_buf]
    st_shape = jax.ShapeDtypeStruct((HEAD, HEAD, 2 * half), F32)
    if mixed:
        assert 2 * half == LANES
        blk = (half, npair, LANES)
        snap = (half, None, HEAD // 2, LANES)
        sf = pl.BlockSpec((N_SHARED,) + blk, lambda l, t: (0, 0, t, 0))
        sb = pl.BlockSpec((N_SHARED,) + blk, lambda l, t: (0, 0, n - 1 - t, 0))
        df = pl.BlockSpec((3, None) + blk, lambda l, t: (0, 0, 0, t, 0))
        db = pl.BlockSpec((3, None) + blk, lambda l, t: (0, 1, 0, n - 1 - t, 0))
        yf_spec = pl.BlockSpec(blk, lambda l, t: (0, t, 0))
        yb_spec = pl.BlockSpec(blk, lambda l, t: (0, n - 1 - t, 0))
        pf_spec = pl.BlockSpec(snap, lambda l, t: (0, t, 0, 0))
        pb_spec = pl.BlockSpec(snap, lambda l, t: (0, n - 1 - t, 0, 0))
        yshape = jax.ShapeDtypeStruct((half, hseq, LANES), F32)
        pshape = jax.ShapeDtypeStruct((half, n, HEAD // 2, LANES), F32)
        yf, yb, pf, pb, stt = pl.pallas_call(
            functools.partial(_scan_kernel, mixed=True, m_tiles=1),
            grid=(1, n),
            in_specs=[sf, df, sb, db, st],
            out_specs=[yf_spec, yb_spec, pf_spec, pb_spec, st],
            out_shape=(yshape, yshape, pshape, pshape, st_shape),
            scratch_shapes=scratch,
            compiler_params=_cparams(("arbitrary", "arbitrary")),
            name="rwkv_scan_mixed",
        )(shared, directed, shared, directed, s0)
        return yf, yb, pf, pb, 0, 0, stt
    assert half % LANES == 0
    m = half // LANES
    chunk = lambda l, t: jnp.where(l >= m, n - 1 - t, t)
    blk = (LANES, npair, LANES)
    sp = pl.BlockSpec((N_SHARED,) + blk, lambda l, t: (0, l % m, chunk(l, t), 0))
    dp = pl.BlockSpec((3, None) + blk, lambda l, t: (0, l // m, l % m, chunk(l, t), 0))
    yp = pl.BlockSpec((None,) + blk, lambda l, t: (l // m, l % m, chunk(l, t), 0))
    pp = pl.BlockSpec((None, LANES, None, HEAD // 2, LANES), lambda l, t: (l // m, l % m, chunk(l, t), 0, 0))
    y, p, stt = pl.pallas_call(
        functools.partial(_scan_kernel, mixed=False, m_tiles=m),
        grid=(2 * m, n),
        in_specs=[sp, dp, st],
        out_specs=[yp, pp, st],
        out_shape=(jax.ShapeDtypeStruct((2, half, hseq, LANES), F32),
                   jax.ShapeDtypeStruct((2, half, n, HEAD // 2, LANES), F32), st_shape),
        scratch_shapes=scratch,
        compiler_params=_cparams(("arbitrary", "arbitrary")),
        name="rwkv_scan",
    )(shared, directed, s0)
    y2 = y.reshape(2 * half, hseq, LANES)
    p2 = p.reshape(2 * half, n, HEAD // 2, LANES)
    return y2, y2, p2, p2, 0, half, stt


def _mix_kernel(saf_ref, sab_ref, pf_ref, pb_ref, rt_ref, bt_ref, kt_ref, v0_ref, v1_ref,
                bonus_ref, g_ref, yh_ref, x_ref, mod_ref, ones_ref, wo_ref, lng_ref, lnb_ref,
                l1g_ref, l1b_ref, o_ref, xs_ref, ps_ref, sa_ref, vs_ref, y_ref):
    rows = x_ref.shape[0]
    c = SCAN_STEPS
    vs_ref[...] = jnp.concatenate([v0_ref[...], v1_ref[...]], axis=1).astype(BF16)
    ti = lax.broadcasted_iota(jnp.int32, (c, 2 * c), 0)
    ii = lax.broadcasted_iota(jnp.int32, (c, 2 * c), 1) & (c - 1)
    keep = (ii <= ti, ii >= ti)
    contract_last = (((1,), (1,)), ((), ()))
    y_ref[...] = jnp.zeros_like(y_ref)
    for d, (sa_paired, p_ref) in enumerate(((saf_ref, pf_ref), (sab_ref, pb_ref))):
        sa_ref[...] = _load_paired(lambda h: sa_paired[h], xs_ref).astype(BF16)

        def chunk(ci, carry):
            r0 = pl.multiple_of(ci * c, c)
            s0 = _load_paired(lambda h: p_ref[h, ci], ps_ref).astype(BF16)
            rt = rt_ref[d, pl.ds(r0, c), :]
            bk = jnp.concatenate([bt_ref[d, pl.ds(r0, c), :], kt_ref[d, pl.ds(r0, c), :]], axis=0)
            sv = jnp.concatenate([sa_ref[pl.ds(r0, c), :], vs_ref[pl.ds(r0, c), :]], axis=0)
            outs = []
            for h in range(HEADS):
                hs = slice(h * HEAD, (h + 1) * HEAD)
                g = lax.dot_general(rt[:, hs], bk[:, hs], contract_last, preferred_element_type=F32)
                g = jnp.where(keep[d], g, 0.0).astype(BF16)
                y0 = lax.dot_general(rt[:, hs], s0[:, hs], contract_last, preferred_element_type=F32)
                outs.append(y0 + _dot(g, sv[:, hs]))
            y_ref[pl.ds(r0, c), :] += jnp.concatenate(outs, axis=1)
            return carry

        lax.fori_loop(0, rows // c, chunk, 0)

    ones = ones_ref[...]
    y = y_ref[...]
    mu = _dot_hi_const(y, ones) * (1.0 / HEAD)
    yc = y - mu
    var = _dot_hi_const(yc * yc, ones) * (1.0 / HEAD)
    yr = yc * lax.rsqrt(var + GN_EPS) * lng_ref[...] + lnb_ref[...] + bonus_ref[...]
    yr = (yr * g_ref[...]).astype(BF16)
    mix = _dot(yh_ref[...], wo_ref[:D_HYENA, :]) + _dot(yr, wo_ref[D_HYENA:, :])
    m = mod_ref[0]
    z = _layernorm_rows(ALPHA * x_ref[...] + m[2:3] * mix)
    o_ref[...] = z * l1g_ref[...] + l1b_ref[...]


def _mix(saf, sab, pf, pb, off_f, off_b, scaled, p3, bonus, g, yh, x, mod, prm):
    batch, seq, _ = x.shape
    tm = PAIR_TILE
    nmod = mod.shape[0]
    nch = tm // SCAN_STEPS
    sa4 = lambda a: a.reshape(-1, HEADS, seq // 2, LANES)
    p5 = lambda a: a.reshape(-1, HEADS, seq // SCAN_STEPS, HEAD // 2, LANES)
    bf, bb = off_f // HEADS, off_b // HEADS
    row = lambda width: pl.BlockSpec((None, tm, width), lambda b, i: (b, i, 0))
    full = lambda a: pl.BlockSpec(a.shape, lambda b, i: (0,) * a.ndim)
    sa_spec = lambda o: pl.BlockSpec((None, HEADS, tm // 2, LANES), lambda b, i: (o + b, 0, i, 0))
    p_spec = lambda o: pl.BlockSpec((None, HEADS, nch, HEAD // 2, LANES), lambda b, i: (o + b, 0, i, 0, 0))
    sc_spec = pl.BlockSpec((2, None, tm, D_RWKV), lambda b, i: (0, b, i, 0))
    cw = MXU_N
    vbase = (3 * D_HYENA + 2 * D_RWKV) // cw
    v_spec = lambda k: pl.BlockSpec((None, tm, cw), lambda b, i: (b, i, vbase + k))
    params = [prm['ones'], prm['w_out'], prm['lnx_g'], prm['lnx_b'], prm['ln1_g'], prm['ln1_b']]
    rt, bt, kt = scaled
    return pl.pallas_call(
        _mix_kernel,
        grid=(batch, seq // tm),
        in_specs=[sa_spec(bf), sa_spec(bb), p_spec(bf), p_spec(bb), sc_spec, sc_spec, sc_spec,
                  v_spec(0), v_spec(1),
                  row(D_RWKV), row(D_RWKV), row(D_HYENA), row(D_MODEL),
                  pl.BlockSpec((1, 6, D_MODEL), lambda b, i: (b % nmod, 0, 0))]
                 + [full(a) for a in params],
        out_specs=row(D_MODEL),
        out_shape=jax.ShapeDtypeStruct((batch, seq, D_MODEL), F32),
        scratch_shapes=[pltpu.VMEM((HEADS // 2, tm, LANES), F32), pltpu.VMEM((HEADS // 2, HEAD, LANES), F32),
                        pltpu.VMEM((tm, D_RWKV), BF16), pltpu.VMEM((tm, D_RWKV), BF16),
                        pltpu.VMEM((tm, D_RWKV), F32)],
        compiler_params=_cparams(("arbitrary", "arbitrary")),
        name="mix_outproj",
    )(sa4(saf), sa4(sab), p5(pf), p5(pb), rt, bt, kt, p3, p3, bonus, g, yh, x, mod, *params)


def _ffn_kernel(x_ref, mod_ref, wu_ref, wg_ref, cw_ref, cb_ref, wd_ref, l2g_ref, l2b_ref, o_ref,
                h_ref, acc_ref, mask_ref, *, seq, on_grid):
    j = pl.program_id(1)
    width = GRID_W if on_grid else seq

    @pl.when(j == 0)
    def _():
        m = mod_ref[0]
        h = _layernorm_rows(x_ref[...]) * (1.0 + m[4:5]) + m[3:4]
        h_ref[...] = h.astype(BF16)
        acc_ref[...] = jnp.zeros_like(acc_ref)
        t = lax.broadcasted_iota(jnp.int32, mask_ref.shape[1:], 0)
        col = t & (width - 1)
        grow = (t & (seq - 1)) >> int(math.log2(GRID_W))
        mask_ref[0] = (col != 0).astype(F32)
        mask_ref[1] = (col != width - 1).astype(F32)
        mask_ref[2] = (grow != 0).astype(F32)
        mask_ref[3] = (grow != seq // GRID_W - 1).astype(F32)

    h = h_ref[...]
    u = _dot(h, wu_ref[...])
    gt = _dot(h, wg_ref[...])
    rows = u.shape[0]
    cw = cw_ref[...]
    wide = lambda k: jnp.concatenate([mask_ref[k]] * (u.shape[1] // LANES), axis=1)
    um = pltpu.roll(u, 1, 0) * wide(0)
    up = pltpu.roll(u, rows - 1, 0) * wide(1)
    if on_grid:
        hrow = [cw[3 * di:3 * di + 1] * um + cw[3 * di + 1:3 * di + 2] * u + cw[3 * di + 2:3 * di + 3] * up
                for di in range(3)]
        u = (hrow[1] + pltpu.roll(hrow[0], GRID_W, 0) * wide(2)
             + pltpu.roll(hrow[2], rows - GRID_W, 0) * wide(3))
    else:
        u = cw[3:4] * um + cw[4:5] * u + cw[5:6] * up
    u = u + cb_ref[...]
    act = 0.5 * u * (1.0 + jnp.tanh(math.sqrt(2.0 / math.pi) * (u + 0.044715 * (u * u * u))))
    acc_ref[...] += _dot((act * gt).astype(BF16), wd_ref[...])

    @pl.when(j == pl.num_programs(1) - 1)
    def _():
        m = mod_ref[0]
        z = _layernorm_rows(ALPHA * x_ref[...] + m[5:6] * acc_ref[...])
        o_ref[...] = z * l2g_ref[...] + l2b_ref[...]


def _ffn(x2, mod, seq, on_grid, prm):
    tokens = x2.shape[0]
    tm = _row_tile(tokens, seq)
    nmod = mod.shape[0]
    assert nmod == 1 or tm <= seq
    assert not on_grid or tm % seq == 0
    tf = MXU_N
    nf = D_FF // tf
    full = lambda a: pl.BlockSpec(a.shape, lambda i, j: (0,) * a.ndim)
    return pl.pallas_call(
        functools.partial(_ffn_kernel, seq=seq, on_grid=on_grid),
        grid=(tokens // tm, nf),
        in_specs=[pl.BlockSpec((tm, D_MODEL), lambda i, j: (i, 0)),
                  pl.BlockSpec((1, 6, D_MODEL), lambda i, j: ((i * tm // seq) % nmod, 0, 0)),
                  pl.BlockSpec((D_MODEL, tf), lambda i, j: (0, j)),
                  pl.BlockSpec((D_MODEL, tf), lambda i, j: (0, nf + j)),
                  pl.BlockSpec((9, tf), lambda i, j: (0, j)),
                  pl.BlockSpec((1, tf), lambda i, j: (0, j)),
                  pl.BlockSpec((tf, D_MODEL), lambda i, j: (j, 0)),
                  full(prm['ln2_g']), full(prm['ln2_b'])],
        out_specs=pl.BlockSpec((tm, D_MODEL), lambda i, j: (i, 0)),
        out_shape=jax.ShapeDtypeStruct((tokens, D_MODEL), F32),
        scratch_shapes=[pltpu.VMEM((tm, D_MODEL), BF16), pltpu.VMEM((tm, D_MODEL), F32),
                        pltpu.VMEM((4, tm, LANES), F32)],
        compiler_params=_cparams(("arbitrary", "arbitrary")),
        name="conv_ffn",
    )(x2, mod, prm['w_ffn_up'], prm['w_ffn_up'], prm['ffn_conv_w'], prm['ffn_conv_b'], prm['w_ffn_down'],
      prm['ln2_g'], prm['ln2_b'])


def _layer(x, mod, s0, on_grid, prm):
    batch, seq, _ = x.shape
    tokens = batch * seq
    p2 = _inproj(x.reshape(tokens, D_MODEL), mod, prm['w_in'], prm['conv_in'], seq)
    p3 = p2.reshape(batch, seq, C_IN)

    spec = _hyena_filter_spectrum(seq, prm['filt_w1'], prm['filt_b1'], prm['filt_w2'], prm['filt_b2'],
                                  prm['filt_w3'], prm['filt_freq'])
    yh = _hyena(p3, spec, prm['hyena_bias'], seq)

    shared, directed, bonus, g, rt, bt, kt = _rwkv_prep(p3, prm)
    half = batch * HEADS
    flat = lambda a: a.reshape(a.shape[:-4] + (half, seq // 2, LANES))
    if s0 is None:
        s0l = jnp.zeros((HEAD, HEAD, 2 * half), F32)
    else:
        s0l = jnp.transpose(s0, (3, 4, 1, 0, 2)).reshape(HEAD, HEAD, 2 * half)
    saf, sab, pf, pb, off_f, off_b, st = _scan(flat(shared), flat(directed), s0l)
    s_final = jnp.transpose(st.reshape(HEAD, HEAD, 2, batch, HEADS), (3, 2, 4, 0, 1))

    x1 = _mix(saf, sab, pf, pb, off_f, off_b, (rt, bt, kt), p3, bonus, g, yh, x, mod, prm)
    out = _ffn(x1.reshape(tokens, D_MODEL), mod, seq, on_grid, prm)
    return out.reshape(batch, seq, D_MODEL), s_final


def kernel(x_prompt, x_sample, state_rwkv, c, c_ctx, ada_w, ada_b, w_in, conv_in, filt_w1, filt_b1, filt_w2, filt_b2, filt_w3, filt_freq, hyena_bias, rwkv_w0, rwkv_w_lora, rwkv_a0, rwkv_a_lora, rwkv_g_lora, rwkv_k_k, rwkv_k_a, rwkv_r_k, lnx_g, lnx_b, w_out, ln1_g, ln1_b, w_ffn_up, ffn_conv_w, ffn_conv_b, w_ffn_down, ln2_g, ln2_b):
    dec_batch = x_sample.shape[0]
    ones = jnp.asarray(_head_ones())
    y_prompt, y_sample = x_prompt, x_sample
    ctx_states = []
    for layer in range(DEPTH):
        row = lambda a: a[layer][None]
        prm = dict(
            ones=ones, w_in=w_in[layer].astype(BF16), conv_in=conv_in[layer],
            filt_w1=filt_w1[layer], filt_b1=filt_b1[layer], filt_w2=filt_w2[layer], filt_b2=filt_b2[layer],
            filt_w3=filt_w3[layer], filt_freq=filt_freq[layer], hyena_bias=row(hyena_bias),
            rwkv_w0=rwkv_w0[layer], rwkv_w_lora=rwkv_w_lora[layer], rwkv_a0=rwkv_a0[layer],
            rwkv_a_lora=rwkv_a_lora[layer], rwkv_g_lora=rwkv_g_lora[layer], rwkv_k_k=row(rwkv_k_k),
            rwkv_k_a=row(rwkv_k_a), rwkv_r_k=rwkv_r_k[layer].reshape(1, D_RWKV),
            lnx_g=row(lnx_g), lnx_b=row(lnx_b), w_out=w_out[layer].astype(BF16),
            ln1_g=row(ln1_g), ln1_b=row(ln1_b), w_ffn_up=w_ffn_up[layer].astype(BF16),
            ffn_conv_w=ffn_conv_w[layer].reshape(9, D_FF), ffn_conv_b=row(ffn_conv_b),
            w_ffn_down=w_ffn_down[layer].astype(BF16), ln2_g=row(ln2_g), ln2_b=row(ln2_b))
        pad = (-(dec_batch + 1)) % SUBLANES
        cond = jnp.concatenate([c, c_ctx[None], jnp.zeros((pad, D_MODEL), F32)], axis=0)
        mod = _ada(cond, ada_w[layer], row(ada_b)).reshape(cond.shape[0], 6, D_MODEL)
        y_prompt, s_ctx = _layer(y_prompt, mod[dec_batch:dec_batch + 1], None, False, prm)
        ctx_states.append(s_ctx)
        y_sample, _ = _layer(y_sample, mod[:dec_batch], state_rwkv[:, layer], True, prm)
    return (y_prompt, y_sample, jnp.stack(ctx_states, axis=1))
```

```python
import functools
import math

import ml_dtypes
import numpy as np
import jax
import jax.numpy as jnp
from jax import lax
from jax.experimental import pallas as pl
from jax.experimental.pallas import tpu as pltpu

F32 = jnp.float32
BF16 = jnp.bfloat16

D_MODEL = 1024
D_HYENA = 512
D_RWKV = 512
HEAD = 64
HEADS = D_RWKV // HEAD
LORA_W = 64
LORA_A = 64
LORA_G = 128
LORA_ALL = LORA_W + LORA_A + LORA_G
C_IN = 3 * D_HYENA + 3 * D_RWKV + LORA_ALL
FILT_BANDS = 16
FILT_FEAT = 1 + 2 * FILT_BANDS
FILT_HIDDEN = 64
N_FILT = 2 * D_HYENA
HYENA_TARGET = 1e-2
HYENA_FAST_PCT = 0.3
HYENA_SLOW_PCT = 1.5
D_FF = 2816
GRID_W = 64
DEPTH = 1
ALPHA = (2.0 * DEPTH) ** 0.25
LN_EPS = 1e-5
GN_EPS = 64e-5
NORM_EPS = 1e-12

LANES = 128
SUBLANES = 8
MXU_N = 256
ROW_TILE = 1024
SCAN_STEPS = 32
VMEM_LIMIT = 56 * 1024 * 1024


def _cparams(sem):
    return pltpu.CompilerParams(dimension_semantics=sem, vmem_limit_bytes=VMEM_LIMIT)


def _dot(a, b):
    return jnp.dot(a, b, preferred_element_type=F32)


def _split(x):
    hi = x.astype(BF16)
    lo = (x - hi.astype(F32)).astype(BF16)
    return hi, lo


def _dot_hi_const(x, c):
    hi, lo = _split(x)
    return _dot(hi, c) + _dot(lo, c)


def _dot3(ah, al, bh, bl):
    return _dot(ah, bh) + _dot(ah, bl) + _dot(al, bh)


def _layernorm_rows(x):
    mu = jnp.mean(x, axis=-1, keepdims=True)
    xc = x - mu
    var = jnp.mean(xc * xc, axis=-1, keepdims=True)
    return xc * lax.rsqrt(var + LN_EPS)


def _row_tile(tokens, seq):
    for tm in (ROW_TILE, 512, 256):
        if tokens % tm == 0 and (tm % seq == 0 or seq % tm == 0):
            return tm
    raise ValueError(f"no row tile for {tokens} tokens of sequence length {seq}")


def _np_split(x64):
    hi = x64.astype(ml_dtypes.bfloat16)
    lo = (x64 - hi.astype(np.float64)).astype(ml_dtypes.bfloat16)
    return hi, lo


@functools.lru_cache(maxsize=None)
def _dft_constants(L):
    N = 2 * L
    f = np.arange(L, dtype=np.int64)[:, None]
    t = np.arange(L, dtype=np.int64)[None, :]
    ang = (2.0 * np.pi / N) * ((f * t) % N).astype(np.float64)
    cos, sin = np.cos(ang), np.sin(ang)
    nyq = np.where(np.arange(L) % 2 == 0, 1.0, -1.0)
    fwd_s = sin.copy()
    fwd_s[0, :] = nyq
    fwd = np.concatenate([cos, fwd_s], axis=0)
    wf = np.full((L, 1), 2.0)
    wf[0, 0] = 1.0
    inv_c = (wf * cos).T / N
    inv_s = (2.0 * sin).T / N
    inv_s[:, 0] = nyq / N
    inv = np.concatenate([inv_c, inv_s], axis=1)
    return _np_split(fwd) + _np_split(inv)


@functools.lru_cache(maxsize=None)
def _filter_constants(L):
    t = np.arange(L, dtype=np.float64)[:, None] / L
    bands = np.arange(1, FILT_BANDS + 1, dtype=np.float64)[None, :]
    ang = (2.0 * math.pi) * bands * t
    feat = np.concatenate([t, np.sin(ang), np.cos(ang)], axis=-1)
    feat = np.pad(feat, ((0, 0), (0, FILT_HIDDEN - FILT_FEAT)))
    slow = abs(math.log(HYENA_TARGET) / HYENA_SLOW_PCT)
    fast = abs(math.log(HYENA_TARGET) / HYENA_FAST_PCT)
    deltas = np.linspace(slow, fast, N_FILT, dtype=np.float64)
    window = np.exp(-t * deltas[None, :])
    return feat.astype(np.float32), window.astype(np.float32)


@functools.lru_cache(maxsize=None)
def _head_ones():
    h = np.arange(D_RWKV) // HEAD
    return (h[:, None] == h[None, :]).astype(ml_dtypes.bfloat16)


def _ada_kernel(c_ref, w_ref, b_ref, o_ref):
    c = c_ref[...]
    s = c * jax.nn.sigmoid(c)
    o_ref[...] = _dot(s.astype(BF16), w_ref[...].astype(BF16)) + b_ref[...]


def _ada(cond, ada_w, ada_b):
    rows = cond.shape[0]
    tn = 1024
    return pl.pallas_call(
        _ada_kernel,
        grid=(6 * D_MODEL // tn,),
        in_specs=[pl.BlockSpec((rows, D_MODEL), lambda j: (0, 0)),
                  pl.BlockSpec((D_MODEL, tn), lambda j: (0, j)),
                  pl.BlockSpec((1, tn), lambda j: (0, j))],
        out_specs=pl.BlockSpec((rows, tn), lambda j: (0, j)),
        out_shape=jax.ShapeDtypeStruct((rows, 6 * D_MODEL), F32),
        compiler_params=_cparams(("arbitrary",)),
        name="ada_mod",
    )(cond, ada_w, ada_b)


def _inproj_kernel(x_ref, mod_ref, w_ref, cw_ref, o_ref, h_ref, mask_ref, *, seq):
    @pl.when(pl.program_id(1) == 0)
    def _():
        m = mod_ref[0]
        h = _layernorm_rows(x_ref[...]) * (1.0 + m[1:2]) + m[0:1]
        h_ref[...] = h.astype(BF16)
        t = lax.broadcasted_iota(jnp.int32, mask_ref.shape[1:], 0) & (seq - 1)
        mask_ref[0] = (t != 0).astype(F32)
        mask_ref[1] = (t != seq - 1).astype(F32)

    p = _dot(h_ref[...], w_ref[...])
    rows = p.shape[0]
    wide = lambda k: jnp.concatenate([mask_ref[k]] * (p.shape[1] // LANES), axis=1)
    prev = pltpu.roll(p, 1, 0) * wide(0)
    nxt = pltpu.roll(p, rows - 1, 0) * wide(1)
    cw = cw_ref[...]
    o_ref[...] = cw[0:1] * prev + cw[1:2] * p + cw[2:3] * nxt


def _inproj(x2, mod, w_in_bf, conv_in, seq):
    tokens = x2.shape[0]
    tm = _row_tile(tokens, seq)
    nmod = mod.shape[0]
    assert nmod == 1 or tm <= seq
    tn = MXU_N
    return pl.pallas_call(
        functools.partial(_inproj_kernel, seq=seq),
        grid=(tokens // tm, C_IN // tn),
        in_specs=[pl.BlockSpec((tm, D_MODEL), lambda i, j: (i, 0)),
                  pl.BlockSpec((1, 6, D_MODEL), lambda i, j: ((i * tm // seq) % nmod, 0, 0)),
                  pl.BlockSpec((D_MODEL, tn), lambda i, j: (0, j)),
                  pl.BlockSpec((3, tn), lambda i, j: (0, j))],
        out_specs=pl.BlockSpec((tm, tn), lambda i, j: (i, j)),
        out_shape=jax.ShapeDtypeStruct((tokens, C_IN), F32),
        scratch_shapes=[pltpu.VMEM((tm, D_MODEL), BF16), pltpu.VMEM((2, tm, LANES), F32)],
        compiler_params=_cparams(("arbitrary", "arbitrary")),
        name="inproj_conv",
    )(x2, mod, w_in_bf, conv_in)


def _filter_kernel(feat_ref, win_ref, w1_ref, b1_ref, w2_ref, b2_ref, w3_ref, fr_ref,
                   fh_ref, fl_ref, a_ref, an_ref, b_ref, *, seq):
    def hp_dot(x, w):
        xh, xl = _split(x)
        wh, wl = _split(w)
        return _dot3(xh, xl, wh, wl)

    fr = fr_ref[...]
    h = jnp.sin(fr[0:1] * (hp_dot(feat_ref[...], w1_ref[...]) + b1_ref[...]))
    h = jnp.sin(fr[1:2] * (hp_dot(h, w2_ref[...]) + b2_ref[...]))
    h = hp_dot(h, w3_ref[...]) * win_ref[...]
    col = jnp.sum(jnp.abs(h), axis=0, keepdims=True)
    inv = 1.0 / (col[:, :D_HYENA] + col[:, D_HYENA:])
    hf = h[:, :D_HYENA] * inv
    hb = h[:, D_HYENA:] * inv
    row = lax.broadcasted_iota(jnp.int32, hf.shape, 0)
    hb = jnp.where(row == 0, 0.0, hb)
    eh, el = _split(hf + hb)
    oh, ol = _split(hf - hb)
    fh, fl = fh_ref[...], fl_ref[...]
    fe = _dot3(fh, fl, eh, el)
    fo = _dot3(fh[seq:], fl[seq:], oh, ol)
    a = fe[:seq]
    a_ref[...] = a
    an_ref[...] = jnp.where(row == 0, fe[seq:seq + 1], a)
    b_ref[...] = jnp.where(row == 0, 0.0, fo)


def _hyena_filter_spectrum(seq, w1, b1, w2, b2, w3, freq):
    feat, window = _filter_constants(seq)
    fh, fl, _, _ = _dft_constants(seq)
    w1p = jnp.pad(w1, ((0, FILT_HIDDEN - FILT_FEAT), (0, 0)))
    shp = jax.ShapeDtypeStruct((seq, D_HYENA), F32)
    return pl.pallas_call(
        functools.partial(_filter_kernel, seq=seq),
        out_shape=(shp, shp, shp),
        compiler_params=pltpu.CompilerParams(vmem_limit_bytes=VMEM_LIMIT),
        name="hyena_filter",
    )(feat, window, w1p, b1[None], w2, b2[None], w3, freq, fh, fl)


def _hyena_kernel(hv_ref, hx0_ref, hx1_ref, fh_ref, fl_ref, gh_ref, gl_ref, a_ref, an_ref, b_ref,
                  bias_ref, o_ref, *, seq):
    z = hx1_ref[...] * hv_ref[...]
    zb = z.astype(BF16)
    zf = _dot(fh_ref[...], zb) + _dot(fl_ref[...], zb)
    zc, zs = zf[:seq], zf[seq:]
    a, an, b = a_ref[...], an_ref[...], b_ref[...]
    yc = zc * a - zs * b
    ys = zc * b + zs * an
    yb = jnp.concatenate([yc, ys], axis=0).astype(BF16)
    y = _dot(gh_ref[...], yb) + _dot(gl_ref[...], yb)
    o_ref[...] = (hx0_ref[...] * (y + bias_ref[...] * z)).astype(o_ref.dtype)


def _hyena(p3, spec, bias, seq):
    batch = p3.shape[0]
    fh, fl, gh, gl = _dft_constants(seq)
    a, an, b = spec
    cb = MXU_N
    ncb = D_HYENA // cb
    const = lambda shape: pl.BlockSpec(shape, lambda c, i: (0, 0), pipeline_mode=pl.Buffered(1))
    col = lambda k: pl.BlockSpec((None, seq, cb), lambda c, i: (i, 0, k * ncb + c))
    chan = lambda rows: pl.BlockSpec((rows, cb), lambda c, i: (0, c))
    return pl.pallas_call(
        functools.partial(_hyena_kernel, seq=seq),
        grid=(ncb, batch),
        in_specs=[col(0), col(1), col(2),
                  const((2 * seq, seq)), const((2 * seq, seq)), const((seq, 2 * seq)), const((seq, 2 * seq)),
                  chan(seq), chan(seq), chan(seq), chan(1)],
        out_specs=pl.BlockSpec((None, seq, cb), lambda c, i: (i, 0, c)),
        out_shape=jax.ShapeDtypeStruct((batch, seq, D_HYENA), BF16),
        compiler_params=_cparams(("arbitrary", "arbitrary")),
        name="hyena_conv",
    )(p3, p3, p3, fh, fl, gh, gl, a, an, b, bias)


PAIR_TILE = 256


def _store_paired(x, xs_ref, put):
    rows = x.shape[0]
    for hp in range(HEADS // 2):
        xs_ref[hp] = x[:, hp * LANES:(hp + 1) * LANES]
    low = lax.broadcasted_iota(jnp.int32, (rows // 2, LANES), 1) < HEAD
    for hp in range(HEADS // 2):
        even = xs_ref[hp, pl.ds(0, rows // 2, stride=2), :]
        odd = xs_ref[hp, pl.ds(1, rows // 2, stride=2), :]
        put(2 * hp, jnp.where(low, even, pltpu.roll(odd, HEAD, 1)))
        put(2 * hp + 1, jnp.where(low, pltpu.roll(even, HEAD, 1), odd))


def _load_paired(get, xs_ref):
    rows = xs_ref.shape[1]
    low = lax.broadcasted_iota(jnp.int32, (rows // 2, LANES), 1) < HEAD
    for hp in range(HEADS // 2):
        b0, b1 = get(2 * hp), get(2 * hp + 1)
        xs_ref[hp, pl.ds(0, rows // 2, stride=2), :] = jnp.where(low, b0, pltpu.roll(b1, HEAD, 1))
        xs_ref[hp, pl.ds(1, rows // 2, stride=2), :] = jnp.where(low, pltpu.roll(b0, HEAD, 1), b1)
    return jnp.concatenate([xs_ref[hp] for hp in range(HEADS // 2)], axis=1)


def _rwkv_prep_kernel(r_ref, k_ref, v_ref, lora_ref, ones_ref, w0_ref, wl_ref, a0_ref, al_ref, gl_ref,
                      kk_ref, ka_ref, rk_ref,
                      sh_ref, dr_ref, bonus_ref, g_ref, xs_ref):
    r, k, v = r_ref[...], k_ref[...], v_ref[...]
    lora = lora_ref[...]
    wd = lora[:, :LORA_W]
    ad = lora[:, LORA_W:LORA_W + LORA_A]
    gd = lora[:, LORA_W + LORA_A:]
    ones = ones_ref[...]

    def shared(q):
        def put(h, val):
            sh_ref[q, h] = val
        return put

    def directed(q, d):
        def put(h, val):
            dr_ref[q, d, h] = val
        return put

    g_ref[...] = _dot(jax.nn.sigmoid(gd).astype(BF16), gl_ref[...].astype(BF16))
    kk = k * kk_ref[...]
    norm = jnp.sqrt(_dot_hi_const(kk * kk, ones))
    kk = kk / jnp.maximum(norm, NORM_EPS)
    _store_paired(r, xs_ref, shared(0))
    _store_paired(v, xs_ref, shared(1))
    _store_paired(-kk, xs_ref, shared(2))
    tw = jnp.tanh(wd).astype(BF16)
    adb = ad.astype(BF16)
    ksum = jnp.zeros_like(k)
    for d in range(2):
        w_raw = w0_ref[d:d + 1] + _dot(tw, wl_ref[d].astype(BF16))
        _store_paired(jnp.exp(-math.exp(-0.5) * jax.nn.sigmoid(w_raw)), xs_ref, directed(0, d))
        a = jax.nn.sigmoid(a0_ref[d:d + 1] + _dot(adb, al_ref[d].astype(BF16)))
        kd = k * (1.0 + (a - 1.0) * ka_ref[...])
        _store_paired(kd, xs_ref, directed(1, d))
        _store_paired(kk * a, xs_ref, directed(2, d))
        ksum = ksum + kd
    bonus_ref[...] = _dot_hi_const(r * ksum * rk_ref[...], ones) * v


def _rwkv_prep(p3, prm):
    batch, seq, _ = p3.shape
    tm = PAIR_TILE
    cbase = 3 * D_HYENA // D_RWKV
    col = lambda k: pl.BlockSpec((None, tm, D_RWKV), lambda b, i: (b, i, cbase + k))
    full = lambda a: pl.BlockSpec(a.shape, lambda b, i: (0,) * a.ndim)
    params = [prm['ones'], prm['rwkv_w0'], prm['rwkv_w_lora'], prm['rwkv_a0'], prm['rwkv_a_lora'],
              prm['rwkv_g_lora'], prm['rwkv_k_k'], prm['rwkv_k_a'], prm['rwkv_r_k']]
    shared = jax.ShapeDtypeStruct((3, batch, HEADS, seq // 2, LANES), F32)
    directed = jax.ShapeDtypeStruct((3, 2, batch, HEADS, seq // 2, LANES), F32)
    rows = jax.ShapeDtypeStruct((batch, seq, D_RWKV), F32)
    shared_spec = pl.BlockSpec((3, None, HEADS, tm // 2, LANES), lambda b, i: (0, b, 0, i, 0))
    directed_spec = pl.BlockSpec((3, 2, None, HEADS, tm // 2, LANES), lambda b, i: (0, 0, b, 0, i, 0))
    rows_spec = pl.BlockSpec((None, tm, D_RWKV), lambda b, i: (b, i, 0))
    return pl.pallas_call(
        _rwkv_prep_kernel,
        grid=(batch, seq // tm),
        in_specs=[col(0), col(1), col(2),
                  pl.BlockSpec((None, tm, LORA_ALL), lambda b, i: (b, i, (C_IN - LORA_ALL) // LORA_ALL))]
                 + [full(a) for a in params],
        out_specs=[shared_spec, directed_spec, rows_spec, rows_spec],
        out_shape=(shared, directed, rows, rows),
        scratch_shapes=[pltpu.VMEM((HEADS // 2, tm, LANES), F32)],
        compiler_params=_cparams(("arbitrary", "arbitrary")),
        name="rwkv_prep",
    )(p3, p3, p3, p3, *params)


N_SCAN_OPERANDS = 6
_R, _VV, _A, _W, _K, _B = range(N_SCAN_OPERANDS)


def _scan_kernel(*refs, mixed, m_tiles):
    if mixed:
        sf_ref, df_ref, sb_ref, db_ref, s0_ref, yf_ref, yb_ref, st_ref = refs[:8]
    else:
        sf_ref, df_ref, s0_ref, y_ref, st_ref = refs[:5]
    s_ref, in_a, in_b, yo_a, yo_b = refs[-5:]
    npair = SCAN_STEPS // 2
    tc = pl.program_id(1)
    backward = pl.program_id(0) >= m_tiles

    @pl.when(tc == 0)
    def _():
        s_ref[...] = s0_ref[...]

    low = lax.broadcasted_iota(jnp.int32, (HEAD, LANES), 1) < HEAD

    def stage(dst, jj):
        src_b = npair - 1 - jj
        src = jnp.where(backward, src_b, jj)
        for q in range(N_SCAN_OPERANDS):
            f_ref, b_ref = (sf_ref, sb_ref if mixed else None) if q < 3 else (df_ref, db_ref if mixed else None)
            qq = q % 3
            if mixed:
                t = jnp.concatenate([f_ref[qq, :, jj, :], b_ref[qq, :, src_b, :]], axis=0).T
                dst[0, q] = jnp.where(low, t[:HEAD], t[HEAD:])
                dst[1, q] = jnp.where(low, t[HEAD:], t[:HEAD])
            else:
                t = f_ref[qq, :, src, :].T
                dst[0, q] = jnp.where(backward, t[HEAD:], t[:HEAD])
                dst[1, q] = jnp.where(backward, t[:HEAD], t[HEAD:])

    def compute(src, yo):
        for sidx in range(2):
            r, w, kk = src[sidx, _R], src[sidx, _W], src[sidx, _K]
            a, b = src[sidx, _A], src[sidx, _B]
            for vb in range(HEAD // SUBLANES):
                base = vb * SUBLANES
                v8 = src[sidx, _VV, base:base + SUBLANES, :]
                ys = []
                for i in range(SUBLANES):
                    sv = s_ref[base + i]
                    sa = jnp.sum(sv * a, axis=0, keepdims=True)
                    sn = sv * w + sa * b + v8[i:i + 1, :] * kk
                    s_ref[base + i] = sn
                    ys.append(jnp.sum(sn * r, axis=0, keepdims=True))
                yo[sidx * HEAD + base:sidx * HEAD + base + SUBLANES, :] = jnp.concatenate(ys, axis=0)

    def emit(yo, jj):
        src_b = npair - 1 - jj
        z = yo[...].T
        if mixed:
            yf_ref[:, jj, :] = z[:HEAD]
            yb_ref[:, src_b, :] = pltpu.roll(z[HEAD:], HEAD, 1)
        else:
            y_ref[:, jnp.where(backward, src_b, jj), :] = jnp.where(backward, pltpu.roll(z, HEAD, 1), z)

    stage(in_a, 0)

    def two_pairs(i, carry):
        j0 = 2 * i
        stage(in_b, j0 + 1)
        compute(in_a, yo_a)
        stage(in_a, jnp.minimum(j0 + 2, npair - 1))
        emit(yo_a, j0)
        compute(in_b, yo_b)
        emit(yo_b, j0 + 1)
        return carry

    lax.fori_loop(0, npair // 2, two_pairs, 0)

    @pl.when(tc == pl.num_programs(1) - 1)
    def _():
        st_ref[...] = s_ref[...]


def _scan(shared, directed, s0):
    _, half, hseq, _ = shared.shape
    npair = SCAN_STEPS // 2
    n = hseq // npair
    mixed = half < LANES
    st = pl.BlockSpec((HEAD, HEAD, LANES), lambda l, t: (0, 0, l))
    stage_buf = pltpu.VMEM((2, N_SCAN_OPERANDS, HEAD, LANES), F32)
    y_buf = pltpu.VMEM((2 * HEAD, LANES), F32)
    scratch = [pltpu.VMEM((HEAD, HEAD, LANES), F32), stage_buf, stage_buf, y_buf, y_buf]
    st_shape = jax.ShapeDtypeStruct((HEAD, HEAD, 2 * half), F32)
    if mixed:
        assert 2 * half == LANES
        blk = (half, npair, LANES)
        sf = pl.BlockSpec((3,) + blk, lambda l, t: (0, 0, t, 0))
        sb = pl.BlockSpec((3,) + blk, lambda l, t: (0, 0, n - 1 - t, 0))
        df = pl.BlockSpec((3, None) + blk, lambda l, t: (0, 0, 0, t, 0))
        db = pl.BlockSpec((3, None) + blk, lambda l, t: (0, 1, 0, n - 1 - t, 0))
        yf_spec = pl.BlockSpec(blk, lambda l, t: (0, t, 0))
        yb_spec = pl.BlockSpec(blk, lambda l, t: (0, n - 1 - t, 0))
        yshape = jax.ShapeDtypeStruct((half, hseq, LANES), F32)
        yf, yb, stt = pl.pallas_call(
            functools.partial(_scan_kernel, mixed=True, m_tiles=1),
            grid=(1, n),
            in_specs=[sf, df, sb, db, st],
            out_specs=[yf_spec, yb_spec, st],
            out_shape=(yshape, yshape, st_shape),
            scratch_shapes=scratch,
            compiler_params=_cparams(("arbitrary", "arbitrary")),
            name="rwkv_scan_mixed",
        )(shared, directed, shared, directed, s0)
        return yf, yb, 0, 0, stt
    assert half % LANES == 0
    m = half // LANES
    chunk = lambda l, t: jnp.where(l >= m, n - 1 - t, t)
    blk = (LANES, npair, LANES)
    sp = pl.BlockSpec((3,) + blk, lambda l, t: (0, l % m, chunk(l, t), 0))
    dp = pl.BlockSpec((3, None) + blk, lambda l, t: (0, l // m, l % m, chunk(l, t), 0))
    yp = pl.BlockSpec((None,) + blk, lambda l, t: (l // m, l % m, chunk(l, t), 0))
    y, stt = pl.pallas_call(
        functools.partial(_scan_kernel, mixed=False, m_tiles=m),
        grid=(2 * m, n),
        in_specs=[sp, dp, st],
        out_specs=[yp, st],
        out_shape=(jax.ShapeDtypeStruct((2, half, hseq, LANES), F32), st_shape),
        scratch_shapes=scratch,
        compiler_params=_cparams(("arbitrary", "arbitrary")),
        name="rwkv_scan",
    )(shared, directed, s0)
    y2 = y.reshape(2 * half, hseq, LANES)
    return y2, y2, 0, half, stt


def _mix_kernel(yf_ref, yb_ref, bonus_ref, g_ref, yh_ref, x_ref, mod_ref, ones_ref, wo_ref, lng_ref, lnb_ref,
                l1g_ref, l1b_ref, o_ref, xs_ref):
    ones = ones_ref[...]
    y = _load_paired(lambda h: yf_ref[h] + yb_ref[h], xs_ref)
    mu = _dot_hi_const(y, ones) * (1.0 / HEAD)
    yc = y - mu
    var = _dot_hi_const(yc * yc, ones) * (1.0 / HEAD)
    yr = yc * lax.rsqrt(var + GN_EPS) * lng_ref[...] + lnb_ref[...] + bonus_ref[...]
    yr = (yr * g_ref[...]).astype(BF16)
    mix = _dot(yh_ref[...], wo_ref[:D_HYENA, :]) + _dot(yr, wo_ref[D_HYENA:, :])
    m = mod_ref[0]
    z = _layernorm_rows(ALPHA * x_ref[...] + m[2:3] * mix)
    o_ref[...] = z * l1g_ref[...] + l1b_ref[...]


def _mix(yf, yb, off_f, off_b, bonus, g, yh, x, mod, prm):
    batch, seq, _ = x.shape
    tm = PAIR_TILE
    nmod = mod.shape[0]
    yf4 = yf.reshape(-1, HEADS, seq // 2, LANES)
    yb4 = yb.reshape(-1, HEADS, seq // 2, LANES)
    bf, bb = off_f // HEADS, off_b // HEADS
    row = lambda width: pl.BlockSpec((None, tm, width), lambda b, i: (b, i, 0))
    full = lambda a: pl.BlockSpec(a.shape, lambda b, i: (0,) * a.ndim)
    params = [prm['ones'], prm['w_out'], prm['lnx_g'], prm['lnx_b'], prm['ln1_g'], prm['ln1_b']]
    return pl.pallas_call(
        _mix_kernel,
        grid=(batch, seq // tm),
        in_specs=[pl.BlockSpec((None, HEADS, tm // 2, LANES), lambda b, i: (bf + b, 0, i, 0)),
                  pl.BlockSpec((None, HEADS, tm // 2, LANES), lambda b, i: (bb + b, 0, i, 0)),
                  row(D_RWKV), row(D_RWKV), row(D_HYENA), row(D_MODEL),
                  pl.BlockSpec((1, 6, D_MODEL), lambda b, i: (b % nmod, 0, 0))]
                 + [full(a) for a in params],
        out_specs=row(D_MODEL),
        out_shape=jax.ShapeDtypeStruct((batch, seq, D_MODEL), F32),
        scratch_shapes=[pltpu.VMEM((HEADS // 2, tm, LANES), F32)],
        compiler_params=_cparams(("arbitrary", "arbitrary")),
        name="mix_outproj",
    )(yf4, yb4, bonus, g, yh, x, mod, *params)


def _ffn_kernel(x_ref, mod_ref, wu_ref, wg_ref, cw_ref, cb_ref, wd_ref, l2g_ref, l2b_ref, o_ref,
                h_ref, acc_ref, mask_ref, *, seq, on_grid):
    j = pl.program_id(1)
    width = GRID_W if on_grid else seq

    @pl.when(j == 0)
    def _():
        m = mod_ref[0]
        h = _layernorm_rows(x_ref[...]) * (1.0 + m[4:5]) + m[3:4]
        h_ref[...] = h.astype(BF16)
        acc_ref[...] = jnp.zeros_like(acc_ref)
        t = lax.broadcasted_iota(jnp.int32, mask_ref.shape[1:], 0)
        col = t & (width - 1)
        grow = (t & (seq - 1)) >> int(math.log2(GRID_W))
        mask_ref[0] = (col != 0).astype(F32)
        mask_ref[1] = (col != width - 1).astype(F32)
        mask_ref[2] = (grow != 0).astype(F32)
        mask_ref[3] = (grow != seq // GRID_W - 1).astype(F32)

    h = h_ref[...]
    u = _dot(h, wu_ref[...])
    rows = u.shape[0]
    cw = cw_ref[...]
    wide = lambda k: jnp.concatenate([mask_ref[k]] * (u.shape[1] // LANES), axis=1)
    um = pltpu.roll(u, 1, 0) * wide(0)
    up = pltpu.roll(u, rows - 1, 0) * wide(1)
    if on_grid:
        hrow = [cw[3 * di:3 * di + 1] * um + cw[3 * di + 1:3 * di + 2] * u + cw[3 * di + 2:3 * di + 3] * up
                for di in range(3)]
        u = (hrow[1] + pltpu.roll(hrow[0], GRID_W, 0) * wide(2)
             + pltpu.roll(hrow[2], rows - GRID_W, 0) * wide(3))
    else:
        u = cw[3:4] * um + cw[4:5] * u + cw[5:6] * up
    u = u + cb_ref[...]
    act = 0.5 * u * (1.0 + jnp.tanh(math.sqrt(2.0 / math.pi) * (u + 0.044715 * (u * u * u))))
    gt = _dot(h, wg_ref[...])
    acc_ref[...] += _dot((act * gt).astype(BF16), wd_ref[...])

    @pl.when(j == pl.num_programs(1) - 1)
    def _():
        m = mod_ref[0]
        z = _layernorm_rows(ALPHA * x_ref[...] + m[5:6] * acc_ref[...])
        o_ref[...] = z * l2g_ref[...] + l2b_ref[...]


def _ffn(x2, mod, seq, on_grid, prm):
    tokens = x2.shape[0]
    tm = _row_tile(tokens, seq)
    nmod = mod.shape[0]
    assert nmod == 1 or tm <= seq
    assert not on_grid or tm % seq == 0
    tf = MXU_N
    nf = D_FF // tf
    full = lambda a: pl.BlockSpec(a.shape, lambda i, j: (0,) * a.ndim)
    return pl.pallas_call(
        functools.partial(_ffn_kernel, seq=seq, on_grid=on_grid),
        grid=(tokens // tm, nf),
        in_specs=[pl.BlockSpec((tm, D_MODEL), lambda i, j: (i, 0)),
                  pl.BlockSpec((1, 6, D_MODEL), lambda i, j: ((i * tm // seq) % nmod, 0, 0)),
                  pl.BlockSpec((D_MODEL, tf), lambda i, j: (0, j)),
                  pl.BlockSpec((D_MODEL, tf), lambda i, j: (0, nf + j)),
                  pl.BlockSpec((9, tf), lambda i, j: (0, j)),
                  pl.BlockSpec((1, tf), lambda i, j: (0, j)),
                  pl.BlockSpec((tf, D_MODEL), lambda i, j: (j, 0)),
                  full(prm['ln2_g']), full(prm['ln2_b'])],
        out_specs=pl.BlockSpec((tm, D_MODEL), lambda i, j: (i, 0)),
        out_shape=jax.ShapeDtypeStruct((tokens, D_MODEL), F32),
        scratch_shapes=[pltpu.VMEM((tm, D_MODEL), BF16), pltpu.VMEM((tm, D_MODEL), F32),
                        pltpu.VMEM((4, tm, LANES), F32)],
        compiler_params=_cparams(("arbitrary", "arbitrary")),
        name="conv_ffn",
    )(x2, mod, prm['w_ffn_up'], prm['w_ffn_up'], prm['ffn_conv_w'], prm['ffn_conv_b'], prm['w_ffn_down'],
      prm['ln2_g'], prm['ln2_b'])


def _layer(x, mod, s0, on_grid, prm):
    batch, seq, _ = x.shape
    tokens = batch * seq
    p2 = _inproj(x.reshape(tokens, D_MODEL), mod, prm['w_in'], prm['conv_in'], seq)
    p3 = p2.reshape(batch, seq, C_IN)

    spec = _hyena_filter_spectrum(seq, prm['filt_w1'], prm['filt_b1'], prm['filt_w2'], prm['filt_b2'],
                                  prm['filt_w3'], prm['filt_freq'])
    yh = _hyena(p3, spec, prm['hyena_bias'], seq)

    shared, directed, bonus, g = _rwkv_prep(p3, prm)
    half = batch * HEADS
    flat = lambda a: a.reshape(a.shape[:-4] + (half, seq // 2, LANES))
    if s0 is None:
        s0l = jnp.zeros((HEAD, HEAD, 2 * half), F32)
    else:
        s0l = jnp.transpose(s0, (3, 4, 1, 0, 2)).reshape(HEAD, HEAD, 2 * half)
    yf, yb, off_f, off_b, st = _scan(flat(shared), flat(directed), s0l)
    s_final = jnp.transpose(st.reshape(HEAD, HEAD, 2, batch, HEADS), (3, 2, 4, 0, 1))

    x1 = _mix(yf, yb, off_f, off_b, bonus, g, yh, x, mod, prm)
    out = _ffn(x1.reshape(tokens, D_MODEL), mod, seq, on_grid, prm)
    return out.reshape(batch, seq, D_MODEL), s_final


def kernel(x_prompt, x_sample, state_rwkv, c, c_ctx, ada_w, ada_b, w_in, conv_in, filt_w1, filt_b1, filt_w2, filt_b2, filt_w3, filt_freq, hyena_bias, rwkv_w0, rwkv_w_lora, rwkv_a0, rwkv_a_lora, rwkv_g_lora, rwkv_k_k, rwkv_k_a, rwkv_r_k, lnx_g, lnx_b, w_out, ln1_g, ln1_b, w_ffn_up, ffn_conv_w, ffn_conv_b, w_ffn_down, ln2_g, ln2_b):
    dec_batch = x_sample.shape[0]
    ones = jnp.asarray(_head_ones())
    y_prompt, y_sample = x_prompt, x_sample
    ctx_states = []
    for layer in range(DEPTH):
        row = lambda a: a[layer][None]
        prm = dict(
            ones=ones, w_in=w_in[layer].astype(BF16), conv_in=conv_in[layer],
            filt_w1=filt_w1[layer], filt_b1=filt_b1[layer], filt_w2=filt_w2[layer], filt_b2=filt_b2[layer],
            filt_w3=filt_w3[layer], filt_freq=filt_freq[layer], hyena_bias=row(hyena_bias),
            rwkv_w0=rwkv_w0[layer], rwkv_w_lora=rwkv_w_lora[layer], rwkv_a0=rwkv_a0[layer],
            rwkv_a_lora=rwkv_a_lora[layer], rwkv_g_lora=rwkv_g_lora[layer], rwkv_k_k=row(rwkv_k_k),
            rwkv_k_a=row(rwkv_k_a), rwkv_r_k=rwkv_r_k[layer].reshape(1, D_RWKV),
            lnx_g=row(lnx_g), lnx_b=row(lnx_b), w_out=w_out[layer].astype(BF16),
            ln1_g=row(ln1_g), ln1_b=row(ln1_b), w_ffn_up=w_ffn_up[layer].astype(BF16),
            ffn_conv_w=ffn_conv_w[layer].reshape(9, D_FF), ffn_conv_b=row(ffn_conv_b),
            w_ffn_down=w_ffn_down[layer].astype(BF16), ln2_g=row(ln2_g), ln2_b=row(ln2_b))
        pad = (-(dec_batch + 1)) % SUBLANES
        cond = jnp.concatenate([c, c_ctx[None], jnp.zeros((pad, D_MODEL), F32)], axis=0)
        mod = _ada(cond, ada_w[layer], row(ada_b)).reshape(cond.shape[0], 6, D_MODEL)
        y_prompt, s_ctx = _layer(y_prompt, mod[dec_batch:dec_batch + 1], None, False, prm)
        ctx_states.append(s_ctx)
        y_sample, _ = _layer(y_sample, mod[:dec_batch], state_rwkv[:, layer], True, prm)
    return (y_prompt, y_sample, jnp.stack(ctx_states, axis=1))
```

```python
import functools
import math

import ml_dtypes
import numpy as np
import jax
import jax.numpy as jnp
from jax import lax
from jax.experimental import pallas as pl
from jax.experimental.pallas import tpu as pltpu

F32 = jnp.float32
BF16 = jnp.bfloat16

D_MODEL = 1024
D_HYENA = 512
D_RWKV = 512
HEAD = 64
HEADS = D_RWKV // HEAD
LORA_W = 64
LORA_A = 64
LORA_G = 128
LORA_ALL = LORA_W + LORA_A + LORA_G
C_IN = 3 * D_HYENA + 3 * D_RWKV + LORA_ALL
FILT_BANDS = 16
FILT_FEAT = 1 + 2 * FILT_BANDS
FILT_HIDDEN = 64
N_FILT = 2 * D_HYENA
HYENA_TARGET = 1e-2
HYENA_FAST_PCT = 0.3
HYENA_SLOW_PCT = 1.5
D_FF = 2816
GRID_W = 64
DEPTH = 1
ALPHA = (2.0 * DEPTH) ** 0.25
LN_EPS = 1e-5
GN_EPS = 64e-5
NORM_EPS = 1e-12

LANES = 128
SUBLANES = 8
MXU_N = 256
ROW_TILE = 1024
SCAN_STEPS = 32
VMEM_LIMIT = 56 * 1024 * 1024


def _cparams(sem):
    return pltpu.CompilerParams(dimension_semantics=sem, vmem_limit_bytes=VMEM_LIMIT)


def _dot(a, b):
    return jnp.dot(a, b, preferred_element_type=F32)


def _split(x):
    hi = x.astype(BF16)
    lo = (x - hi.astype(F32)).astype(BF16)
    return hi, lo


def _dot_hi_const(x, c):
    hi, lo = _split(x)
    return _dot(hi, c) + _dot(lo, c)


def _dot3(ah, al, bh, bl):
    return _dot(ah, bh) + _dot(ah, bl) + _dot(al, bh)


def _layernorm_rows(x):
    mu = jnp.mean(x, axis=-1, keepdims=True)
    xc = x - mu
    var = jnp.mean(xc * xc, axis=-1, keepdims=True)
    return xc * lax.rsqrt(var + LN_EPS)


def _row_tile(tokens, seq):
    for tm in (ROW_TILE, 512, 256):
        if tokens % tm == 0 and (tm % seq == 0 or seq % tm == 0):
            return tm
    raise ValueError(f"no row tile for {tokens} tokens of sequence length {seq}")


def _np_split(x64):
    hi = x64.astype(ml_dtypes.bfloat16)
    lo = (x64 - hi.astype(np.float64)).astype(ml_dtypes.bfloat16)
    return hi, lo


@functools.lru_cache(maxsize=None)
def _dft_constants(L):
    N = 2 * L
    f = np.arange(L, dtype=np.int64)[:, None]
    t = np.arange(L, dtype=np.int64)[None, :]
    ang = (2.0 * np.pi / N) * ((f * t) % N).astype(np.float64)
    cos, sin = np.cos(ang), np.sin(ang)
    nyq = np.where(np.arange(L) % 2 == 0, 1.0, -1.0)
    fwd_s = sin.copy()
    fwd_s[0, :] = nyq
    fwd = np.concatenate([cos, fwd_s], axis=0)
    wf = np.full((L, 1), 2.0)
    wf[0, 0] = 1.0
    inv_c = (wf * cos).T / N
    inv_s = (2.0 * sin).T / N
    inv_s[:, 0] = nyq / N
    inv = np.concatenate([inv_c, inv_s], axis=1)
    return _np_split(fwd) + _np_split(inv)


@functools.lru_cache(maxsize=None)
def _filter_constants(L):
    t = np.arange(L, dtype=np.float64)[:, None] / L
    bands = np.arange(1, FILT_BANDS + 1, dtype=np.float64)[None, :]
    ang = (2.0 * math.pi) * bands * t
    feat = np.concatenate([t, np.sin(ang), np.cos(ang)], axis=-1)
    feat = np.pad(feat, ((0, 0), (0, FILT_HIDDEN - FILT_FEAT)))
    slow = abs(math.log(HYENA_TARGET) / HYENA_SLOW_PCT)
    fast = abs(math.log(HYENA_TARGET) / HYENA_FAST_PCT)
    deltas = np.linspace(slow, fast, N_FILT, dtype=np.float64)
    window = np.exp(-t * deltas[None, :])
    return feat.astype(np.float32), window.astype(np.float32)


@functools.lru_cache(maxsize=None)
def _head_ones():
    h = np.arange(D_RWKV) // HEAD
    return (h[:, None] == h[None, :]).astype(ml_dtypes.bfloat16)


def _ada_kernel(c_ref, w_ref, b_ref, o_ref):
    c = c_ref[...]
    s = c * jax.nn.sigmoid(c)
    o_ref[...] = _dot(s.astype(BF16), w_ref[...].astype(BF16)) + b_ref[...]


def _ada(cond, ada_w, ada_b):
    rows = cond.shape[0]
    tn = 1024
    return pl.pallas_call(
        _ada_kernel,
        grid=(6 * D_MODEL // tn,),
        in_specs=[pl.BlockSpec((rows, D_MODEL), lambda j: (0, 0)),
                  pl.BlockSpec((D_MODEL, tn), lambda j: (0, j)),
                  pl.BlockSpec((1, tn), lambda j: (0, j))],
        out_specs=pl.BlockSpec((rows, tn), lambda j: (0, j)),
        out_shape=jax.ShapeDtypeStruct((rows, 6 * D_MODEL), F32),
        compiler_params=_cparams(("arbitrary",)),
        name="ada_mod",
    )(cond, ada_w, ada_b)


def _inproj_kernel(x_ref, mod_ref, w_ref, cw_ref, o_ref, h_ref, mask_ref, *, seq):
    @pl.when(pl.program_id(1) == 0)
    def _():
        m = mod_ref[0]
        h = _layernorm_rows(x_ref[...]) * (1.0 + m[1:2]) + m[0:1]
        h_ref[...] = h.astype(BF16)
        t = lax.broadcasted_iota(jnp.int32, mask_ref.shape[1:], 0) & (seq - 1)
        mask_ref[0] = (t != 0).astype(F32)
        mask_ref[1] = (t != seq - 1).astype(F32)

    p = _dot(h_ref[...], w_ref[...])
    rows = p.shape[0]
    wide = lambda k: jnp.concatenate([mask_ref[k]] * (p.shape[1] // LANES), axis=1)
    prev = pltpu.roll(p, 1, 0) * wide(0)
    nxt = pltpu.roll(p, rows - 1, 0) * wide(1)
    cw = cw_ref[...]
    o_ref[...] = cw[0:1] * prev + cw[1:2] * p + cw[2:3] * nxt


def _inproj(x2, mod, w_in_bf, conv_in, seq):
    tokens = x2.shape[0]
    tm = _row_tile(tokens, seq)
    nmod = mod.shape[0]
    assert nmod == 1 or tm <= seq
    tn = MXU_N
    return pl.pallas_call(
        functools.partial(_inproj_kernel, seq=seq),
        grid=(tokens // tm, C_IN // tn),
        in_specs=[pl.BlockSpec((tm, D_MODEL), lambda i, j: (i, 0)),
                  pl.BlockSpec((1, 6, D_MODEL), lambda i, j: ((i * tm // seq) % nmod, 0, 0)),
                  pl.BlockSpec((D_MODEL, tn), lambda i, j: (0, j)),
                  pl.BlockSpec((3, tn), lambda i, j: (0, j))],
        out_specs=pl.BlockSpec((tm, tn), lambda i, j: (i, j)),
        out_shape=jax.ShapeDtypeStruct((tokens, C_IN), F32),
        scratch_shapes=[pltpu.VMEM((tm, D_MODEL), BF16), pltpu.VMEM((2, tm, LANES), F32)],
        compiler_params=_cparams(("arbitrary", "arbitrary")),
        name="inproj_conv",
    )(x2, mod, w_in_bf, conv_in)


def _filter_kernel(feat_ref, win_ref, w1_ref, b1_ref, w2_ref, b2_ref, w3_ref, fr_ref,
                   fh_ref, fl_ref, a_ref, an_ref, b_ref, *, seq):
    def hp_dot(x, w):
        xh, xl = _split(x)
        wh, wl = _split(w)
        return _dot3(xh, xl, wh, wl)

    fr = fr_ref[...]
    h = jnp.sin(fr[0:1] * (hp_dot(feat_ref[...], w1_ref[...]) + b1_ref[...]))
    h = jnp.sin(fr[1:2] * (hp_dot(h, w2_ref[...]) + b2_ref[...]))
    h = hp_dot(h, w3_ref[...]) * win_ref[...]
    col = jnp.sum(jnp.abs(h), axis=0, keepdims=True)
    inv = 1.0 / (col[:, :D_HYENA] + col[:, D_HYENA:])
    hf = h[:, :D_HYENA] * inv
    hb = h[:, D_HYENA:] * inv
    row = lax.broadcasted_iota(jnp.int32, hf.shape, 0)
    hb = jnp.where(row == 0, 0.0, hb)
    eh, el = _split(hf + hb)
    oh, ol = _split(hf - hb)
    fh, fl = fh_ref[...], fl_ref[...]
    fe = _dot3(fh, fl, eh, el)
    fo = _dot3(fh[seq:], fl[seq:], oh, ol)
    a = fe[:seq]
    a_ref[...] = a
    an_ref[...] = jnp.where(row == 0, fe[seq:seq + 1], a)
    b_ref[...] = jnp.where(row == 0, 0.0, fo)


def _hyena_filter_spectrum(seq, w1, b1, w2, b2, w3, freq):
    feat, window = _filter_constants(seq)
    fh, fl, _, _ = _dft_constants(seq)
    w1p = jnp.pad(w1, ((0, FILT_HIDDEN - FILT_FEAT), (0, 0)))
    shp = jax.ShapeDtypeStruct((seq, D_HYENA), F32)
    return pl.pallas_call(
        functools.partial(_filter_kernel, seq=seq),
        out_shape=(shp, shp, shp),
        compiler_params=pltpu.CompilerParams(vmem_limit_bytes=VMEM_LIMIT),
        name="hyena_filter",
    )(feat, window, w1p, b1[None], w2, b2[None], w3, freq, fh, fl)


def _hyena_kernel(hv_ref, hx0_ref, hx1_ref, fh_ref, fl_ref, gh_ref, gl_ref, a_ref, an_ref, b_ref,
                  bias_ref, o_ref, *, seq):
    z = hx1_ref[...] * hv_ref[...]
    zb = z.astype(BF16)
    zf = _dot(fh_ref[...], zb) + _dot(fl_ref[...], zb)
    zc, zs = zf[:seq], zf[seq:]
    a, an, b = a_ref[...], an_ref[...], b_ref[...]
    yc = zc * a - zs * b
    ys = zc * b + zs * an
    yb = jnp.concatenate([yc, ys], axis=0).astype(BF16)
    y = _dot(gh_ref[...], yb) + _dot(gl_ref[...], yb)
    o_ref[...] = (hx0_ref[...] * (y + bias_ref[...] * z)).astype(o_ref.dtype)


def _hyena(p3, spec, bias, seq):
    batch = p3.shape[0]
    fh, fl, gh, gl = _dft_constants(seq)
    a, an, b = spec
    cb = MXU_N
    ncb = D_HYENA // cb
    const = lambda shape: pl.BlockSpec(shape, lambda c, i: (0, 0), pipeline_mode=pl.Buffered(1))
    col = lambda k: pl.BlockSpec((None, seq, cb), lambda c, i: (i, 0, k * ncb + c))
    chan = lambda rows: pl.BlockSpec((rows, cb), lambda c, i: (0, c))
    return pl.pallas_call(
        functools.partial(_hyena_kernel, seq=seq),
        grid=(ncb, batch),
        in_specs=[col(0), col(1), col(2),
                  const((2 * seq, seq)), const((2 * seq, seq)), const((seq, 2 * seq)), const((seq, 2 * seq)),
                  chan(seq), chan(seq), chan(seq), chan(1)],
        out_specs=pl.BlockSpec((None, seq, cb), lambda c, i: (i, 0, c)),
        out_shape=jax.ShapeDtypeStruct((batch, seq, D_HYENA), BF16),
        compiler_params=_cparams(("arbitrary", "arbitrary")),
        name="hyena_conv",
    )(p3, p3, p3, fh, fl, gh, gl, a, an, b, bias)


PAIR_TILE = 256


def _store_paired(x, xs_ref, put):
    rows = x.shape[0]
    for hp in range(HEADS // 2):
        xs_ref[hp] = x[:, hp * LANES:(hp + 1) * LANES]
    low = lax.broadcasted_iota(jnp.int32, (rows // 2, LANES), 1) < HEAD
    for hp in range(HEADS // 2):
        even = xs_ref[hp, pl.ds(0, rows // 2, stride=2), :]
        odd = xs_ref[hp, pl.ds(1, rows // 2, stride=2), :]
        put(2 * hp, jnp.where(low, even, pltpu.roll(odd, HEAD, 1)))
        put(2 * hp + 1, jnp.where(low, pltpu.roll(even, HEAD, 1), odd))


def _load_paired(get, xs_ref):
    rows = xs_ref.shape[1]
    low = lax.broadcasted_iota(jnp.int32, (rows // 2, LANES), 1) < HEAD
    for hp in range(HEADS // 2):
        b0, b1 = get(2 * hp), get(2 * hp + 1)
        xs_ref[hp, pl.ds(0, rows // 2, stride=2), :] = jnp.where(low, b0, pltpu.roll(b1, HEAD, 1))
        xs_ref[hp, pl.ds(1, rows // 2, stride=2), :] = jnp.where(low, pltpu.roll(b0, HEAD, 1), b1)
    return jnp.concatenate([xs_ref[hp] for hp in range(HEADS // 2)], axis=1)


def _rwkv_prep_kernel(r_ref, k_ref, v_ref, lora_ref, ones_ref, w0_ref, wl_ref, a0_ref, al_ref, gl_ref,
                      kk_ref, ka_ref, rk_ref,
                      sh_ref, dr_ref, bonus_ref, g_ref, xs_ref):
    r, k, v = r_ref[...], k_ref[...], v_ref[...]
    lora = lora_ref[...]
    wd = lora[:, :LORA_W]
    ad = lora[:, LORA_W:LORA_W + LORA_A]
    gd = lora[:, LORA_W + LORA_A:]
    ones = ones_ref[...]

    def shared(q):
        def put(h, val):
            sh_ref[q, h] = val
        return put

    def directed(q, d):
        def put(h, val):
            dr_ref[q, d, h] = val
        return put

    g_ref[...] = _dot(jax.nn.sigmoid(gd).astype(BF16), gl_ref[...].astype(BF16))
    kk = k * kk_ref[...]
    norm = jnp.sqrt(_dot_hi_const(kk * kk, ones))
    kk = kk / jnp.maximum(norm, NORM_EPS)
    _store_paired(r, xs_ref, shared(0))
    _store_paired(v, xs_ref, shared(1))
    _store_paired(-kk, xs_ref, shared(2))
    tw = jnp.tanh(wd).astype(BF16)
    adb = ad.astype(BF16)
    ksum = jnp.zeros_like(k)
    for d in range(2):
        w_raw = w0_ref[d:d + 1] + _dot(tw, wl_ref[d].astype(BF16))
        _store_paired(jnp.exp(-math.exp(-0.5) * jax.nn.sigmoid(w_raw)), xs_ref, directed(0, d))
        a = jax.nn.sigmoid(a0_ref[d:d + 1] + _dot(adb, al_ref[d].astype(BF16)))
        kd = k * (1.0 + (a - 1.0) * ka_ref[...])
        _store_paired(kd, xs_ref, directed(1, d))
        _store_paired(kk * a, xs_ref, directed(2, d))
        ksum = ksum + kd
    bonus_ref[...] = _dot_hi_const(r * ksum * rk_ref[...], ones) * v


def _rwkv_prep(p3, prm):
    batch, seq, _ = p3.shape
    tm = PAIR_TILE
    cbase = 3 * D_HYENA // D_RWKV
    col = lambda k: pl.BlockSpec((None, tm, D_RWKV), lambda b, i: (b, i, cbase + k))
    full = lambda a: pl.BlockSpec(a.shape, lambda b, i: (0,) * a.ndim)
    params = [prm['ones'], prm['rwkv_w0'], prm['rwkv_w_lora'], prm['rwkv_a0'], prm['rwkv_a_lora'],
              prm['rwkv_g_lora'], prm['rwkv_k_k'], prm['rwkv_k_a'], prm['rwkv_r_k']]
    shared = jax.ShapeDtypeStruct((3, batch, HEADS, seq // 2, LANES), F32)
    directed = jax.ShapeDtypeStruct((3, 2, batch, HEADS, seq // 2, LANES), F32)
    rows = jax.ShapeDtypeStruct((batch, seq, D_RWKV), F32)
    shared_spec = pl.BlockSpec((3, None, HEADS, tm // 2, LANES), lambda b, i: (0, b, 0, i, 0))
    directed_spec = pl.BlockSpec((3, 2, None, HEADS, tm // 2, LANES), lambda b, i: (0, 0, b, 0, i, 0))
    rows_spec = pl.BlockSpec((None, tm, D_RWKV), lambda b, i: (b, i, 0))
    return pl.pallas_call(
        _rwkv_prep_kernel,
        grid=(batch, seq // tm),
        in_specs=[col(0), col(1), col(2),
                  pl.BlockSpec((None, tm, LORA_ALL), lambda b, i: (b, i, (C_IN - LORA_ALL) // LORA_ALL))]
                 + [full(a) for a in params],
        out_specs=[shared_spec, directed_spec, rows_spec, rows_spec],
        out_shape=(shared, directed, rows, rows),
        scratch_shapes=[pltpu.VMEM((HEADS // 2, tm, LANES), F32)],
        compiler_params=_cparams(("arbitrary", "arbitrary")),
        name="rwkv_prep",
    )(p3, p3, p3, p3, *params)


N_SCAN_OPERANDS = 6
_R, _VV, _A, _W, _K, _B = range(N_SCAN_OPERANDS)


def _scan_kernel(*refs, mixed, m_tiles):
    if mixed:
        sf_ref, df_ref, sb_ref, db_ref, s0_ref, yf_ref, yb_ref, st_ref = refs[:8]
    else:
        sf_ref, df_ref, s0_ref, y_ref, st_ref = refs[:5]
    s_ref, in_a, in_b, yo_a, yo_b = refs[-5:]
    npair = SCAN_STEPS // 2
    tc = pl.program_id(1)
    backward = pl.program_id(0) >= m_tiles

    @pl.when(tc == 0)
    def _():
        s_ref[...] = s0_ref[...]

    low = lax.broadcasted_iota(jnp.int32, (HEAD, LANES), 1) < HEAD

    def stage(dst, jj):
        src_b = npair - 1 - jj
        src = jnp.where(backward, src_b, jj)
        for q in range(N_SCAN_OPERANDS):
            f_ref, b_ref = (sf_ref, sb_ref if mixed else None) if q < 3 else (df_ref, db_ref if mixed else None)
            qq = q % 3
            if mixed:
                t = jnp.concatenate([f_ref[qq, :, jj, :], b_ref[qq, :, src_b, :]], axis=0).T
                dst[0, q] = jnp.where(low, t[:HEAD], t[HEAD:])
                dst[1, q] = jnp.where(low, t[HEAD:], t[:HEAD])
            else:
                t = f_ref[qq, :, src, :].T
                dst[0, q] = jnp.where(backward, t[HEAD:], t[:HEAD])
                dst[1, q] = jnp.where(backward, t[:HEAD], t[HEAD:])

    sub = lax.broadcasted_iota(jnp.int32, (SUBLANES, LANES), 0)

    def block_sum(x):
        acc = x[0:SUBLANES]
        for j in range(1, HEAD // SUBLANES):
            acc = acc + x[j * SUBLANES:(j + 1) * SUBLANES]
        return acc

    def all_sum(p):
        p = p + pltpu.roll(p, 4, 0)
        p = p + pltpu.roll(p, 2, 0)
        return p + pltpu.roll(p, 1, 0)

    def fold(x, y, sh):
        m = (sub & sh) == 0
        return jnp.where(m, x, pltpu.roll(y, sh, 0)) + jnp.where(m, pltpu.roll(x, SUBLANES - sh, 0), y)

    def rows_sum(ps):
        q = [ps[j] for j in (0, 4, 2, 6, 1, 5, 3, 7)]
        s4 = [fold(q[2 * i], q[2 * i + 1], 4) for i in range(4)]
        return fold(fold(s4[0], s4[1], 2), fold(s4[2], s4[3], 2), 1)

    def compute(src, yo):
        for sidx in range(2):
            r, w, kk = src[sidx, _R], src[sidx, _W], src[sidx, _K]
            a, b = src[sidx, _A], src[sidx, _B]
            for vb in range(HEAD // SUBLANES):
                base = vb * SUBLANES
                v8 = src[sidx, _VV, base:base + SUBLANES, :]
                ps = []
                for i in range(SUBLANES):
                    sv = s_ref[base + i]
                    sa = jnp.tile(all_sum(block_sum(sv * a)), (HEAD // SUBLANES, 1))
                    sn = sv * w + sa * b + v8[i:i + 1, :] * kk
                    s_ref[base + i] = sn
                    ps.append(block_sum(sn * r))
                yo[sidx * HEAD + base:sidx * HEAD + base + SUBLANES, :] = rows_sum(ps)

    def emit(yo, jj):
        src_b = npair - 1 - jj
        z = yo[...].T
        if mixed:
            yf_ref[:, jj, :] = z[:HEAD]
            yb_ref[:, src_b, :] = pltpu.roll(z[HEAD:], HEAD, 1)
        else:
            y_ref[:, jnp.where(backward, src_b, jj), :] = jnp.where(backward, pltpu.roll(z, HEAD, 1), z)

    stage(in_a, 0)

    def two_pairs(i, carry):
        j0 = 2 * i
        stage(in_b, j0 + 1)
        compute(in_a, yo_a)
        stage(in_a, jnp.minimum(j0 + 2, npair - 1))
        emit(yo_a, j0)
        compute(in_b, yo_b)
        emit(yo_b, j0 + 1)
        return carry

    lax.fori_loop(0, npair // 2, two_pairs, 0)

    @pl.when(tc == pl.num_programs(1) - 1)
    def _():
        st_ref[...] = s_ref[...]


def _scan(shared, directed, s0):
    _, half, hseq, _ = shared.shape
    npair = SCAN_STEPS // 2
    n = hseq // npair
    mixed = half < LANES
    st = pl.BlockSpec((HEAD, HEAD, LANES), lambda l, t: (0, 0, l))
    stage_buf = pltpu.VMEM((2, N_SCAN_OPERANDS, HEAD, LANES), F32)
    y_buf = pltpu.VMEM((2 * HEAD, LANES), F32)
    scratch = [pltpu.VMEM((HEAD, HEAD, LANES), F32), stage_buf, stage_buf, y_buf, y_buf]
    st_shape = jax.ShapeDtypeStruct((HEAD, HEAD, 2 * half), F32)
    if mixed:
        assert 2 * half == LANES
        blk = (half, npair, LANES)
        sf = pl.BlockSpec((3,) + blk, lambda l, t: (0, 0, t, 0))
        sb = pl.BlockSpec((3,) + blk, lambda l, t: (0, 0, n - 1 - t, 0))
        df = pl.BlockSpec((3, None) + blk, lambda l, t: (0, 0, 0, t, 0))
        db = pl.BlockSpec((3, None) + blk, lambda l, t: (0, 1, 0, n - 1 - t, 0))
        yf_spec = pl.BlockSpec(blk, lambda l, t: (0, t, 0))
        yb_spec = pl.BlockSpec(blk, lambda l, t: (0, n - 1 - t, 0))
        yshape = jax.ShapeDtypeStruct((half, hseq, LANES), F32)
        yf, yb, stt = pl.pallas_call(
            functools.partial(_scan_kernel, mixed=True, m_tiles=1),
            grid=(1, n),
            in_specs=[sf, df, sb, db, st],
            out_specs=[yf_spec, yb_spec, st],
            out_shape=(yshape, yshape, st_shape),
            scratch_shapes=scratch,
            compiler_params=_cparams(("arbitrary", "arbitrary")),
            name="rwkv_scan_mixed",
        )(shared, directed, shared, directed, s0)
        return yf, yb, 0, 0, stt
    assert half % LANES == 0
    m = half // LANES
    chunk = lambda l, t: jnp.where(l >= m, n - 1 - t, t)
    blk = (LANES, npair, LANES)
    sp = pl.BlockSpec((3,) + blk, lambda l, t: (0, l % m, chunk(l, t), 0))
    dp = pl.BlockSpec((3, None) + blk, lambda l, t: (0, l // m, l % m, chunk(l, t), 0))
    yp = pl.BlockSpec((None,) + blk, lambda l, t: (l // m, l % m, chunk(l, t), 0))
    y, stt = pl.pallas_call(
        functools.partial(_scan_kernel, mixed=False, m_tiles=m),
        grid=(2 * m, n),
        in_specs=[sp, dp, st],
        out_specs=[yp, st],
        out_shape=(jax.ShapeDtypeStruct((2, half, hseq, LANES), F32), st_shape),
        scratch_shapes=scratch,
        compiler_params=_cparams(("arbitrary", "arbitrary")),
        name="rwkv_scan",
    )(shared, directed, s0)
    y2 = y.reshape(2 * half, hseq, LANES)
    return y2, y2, 0, half, stt


def _mix_kernel(yf_ref, yb_ref, bonus_ref, g_ref, yh_ref, x_ref, mod_ref, ones_ref, wo_ref, lng_ref, lnb_ref,
                l1g_ref, l1b_ref, o_ref, xs_ref):
    ones = ones_ref[...]
    y = _load_paired(lambda h: yf_ref[h] + yb_ref[h], xs_ref)
    mu = _dot_hi_const(y, ones) * (1.0 / HEAD)
    yc = y - mu
    var = _dot_hi_const(yc * yc, ones) * (1.0 / HEAD)
    yr = yc * lax.rsqrt(var + GN_EPS) * lng_ref[...] + lnb_ref[...] + bonus_ref[...]
    yr = (yr * g_ref[...]).astype(BF16)
    mix = _dot(yh_ref[...], wo_ref[:D_HYENA, :]) + _dot(yr, wo_ref[D_HYENA:, :])
    m = mod_ref[0]
    z = _layernorm_rows(ALPHA * x_ref[...] + m[2:3] * mix)
    o_ref[...] = z * l1g_ref[...] + l1b_ref[...]


def _mix(yf, yb, off_f, off_b, bonus, g, yh, x, mod, prm):
    batch, seq, _ = x.shape
    tm = PAIR_TILE
    nmod = mod.shape[0]
    yf4 = yf.reshape(-1, HEADS, seq // 2, LANES)
    yb4 = yb.reshape(-1, HEADS, seq // 2, LANES)
    bf, bb = off_f // HEADS, off_b // HEADS
    row = lambda width: pl.BlockSpec((None, tm, width), lambda b, i: (b, i, 0))
    full = lambda a: pl.BlockSpec(a.shape, lambda b, i: (0,) * a.ndim)
    params = [prm['ones'], prm['w_out'], prm['lnx_g'], prm['lnx_b'], prm['ln1_g'], prm['ln1_b']]
    return pl.pallas_call(
        _mix_kernel,
        grid=(batch, seq // tm),
        in_specs=[pl.BlockSpec((None, HEADS, tm // 2, LANES), lambda b, i: (bf + b, 0, i, 0)),
                  pl.BlockSpec((None, HEADS, tm // 2, LANES), lambda b, i: (bb + b, 0, i, 0)),
                  row(D_RWKV), row(D_RWKV), row(D_HYENA), row(D_MODEL),
                  pl.BlockSpec((1, 6, D_MODEL), lambda b, i: (b % nmod, 0, 0))]
                 + [full(a) for a in params],
        out_specs=row(D_MODEL),
        out_shape=jax.ShapeDtypeStruct((batch, seq, D_MODEL), F32),
        scratch_shapes=[pltpu.VMEM((HEADS // 2, tm, LANES), F32)],
        compiler_params=_cparams(("arbitrary", "arbitrary")),
        name="mix_outproj",
    )(yf4, yb4, bonus, g, yh, x, mod, *params)


def _ffn_kernel(x_ref, mod_ref, wu_ref, wg_ref, cw_ref, cb_ref, wd_ref, l2g_ref, l2b_ref, o_ref,
                h_ref, acc_ref, mask_ref, *, seq, on_grid):
    j = pl.program_id(1)
    width = GRID_W if on_grid else seq

    @pl.when(j == 0)
    def _():
        m = mod_ref[0]
        h = _layernorm_rows(x_ref[...]) * (1.0 + m[4:5]) + m[3:4]
        h_ref[...] = h.astype(BF16)
        acc_ref[...] = jnp.zeros_like(acc_ref)
        t = lax.broadcasted_iota(jnp.int32, mask_ref.shape[1:], 0)
        col = t & (width - 1)
        grow = (t & (seq - 1)) >> int(math.log2(GRID_W))
        mask_ref[0] = (col != 0).astype(F32)
        mask_ref[1] = (col != width - 1).astype(F32)
        mask_ref[2] = (grow != 0).astype(F32)
        mask_ref[3] = (grow != seq // GRID_W - 1).astype(F32)

    h = h_ref[...]
    u = _dot(h, wu_ref[...])
    rows = u.shape[0]
    cw = cw_ref[...]
    wide = lambda k: jnp.concatenate([mask_ref[k]] * (u.shape[1] // LANES), axis=1)
    um = pltpu.roll(u, 1, 0) * wide(0)
    up = pltpu.roll(u, rows - 1, 0) * wide(1)
    if on_grid:
        hrow = [cw[3 * di:3 * di + 1] * um + cw[3 * di + 1:3 * di + 2] * u + cw[3 * di + 2:3 * di + 3] * up
                for di in range(3)]
        if rows == seq:
            pad = jnp.zeros((GRID_W, u.shape[1]), F32)
            u = (hrow[1] + jnp.concatenate([pad, hrow[0][:rows - GRID_W]], axis=0)
                 + jnp.concatenate([hrow[2][GRID_W:], pad], axis=0))
        else:
            u = (hrow[1] + pltpu.roll(hrow[0], GRID_W, 0) * wide(2)
                 + pltpu.roll(hrow[2], rows - GRID_W, 0) * wide(3))
    else:
        u = cw[3:4] * um + cw[4:5] * u + cw[5:6] * up
    u = u + cb_ref[...]
    c0 = math.sqrt(2.0 / math.pi)
    act = u + u * jnp.tanh(u * (c0 + (c0 * 0.044715) * (u * u)))
    gt = _dot(h, wg_ref[...])
    acc_ref[...] += _dot((act * gt).astype(BF16), wd_ref[...])

    @pl.when(j == pl.num_programs(1) - 1)
    def _():
        m = mod_ref[0]
        z = _layernorm_rows(ALPHA * x_ref[...] + m[5:6] * acc_ref[...])
        o_ref[...] = z * l2g_ref[...] + l2b_ref[...]


def _ffn(x2, mod, seq, on_grid, prm):
    tokens = x2.shape[0]
    tm = _row_tile(tokens, seq)
    nmod = mod.shape[0]
    assert nmod == 1 or tm <= seq
    assert not on_grid or tm % seq == 0
    tf = MXU_N
    nf = D_FF // tf
    full = lambda a: pl.BlockSpec(a.shape, lambda i, j: (0,) * a.ndim)
    return pl.pallas_call(
        functools.partial(_ffn_kernel, seq=seq, on_grid=on_grid),
        grid=(tokens // tm, nf),
        in_specs=[pl.BlockSpec((tm, D_MODEL), lambda i, j: (i, 0)),
                  pl.BlockSpec((1, 6, D_MODEL), lambda i, j: ((i * tm // seq) % nmod, 0, 0)),
                  pl.BlockSpec((D_MODEL, tf), lambda i, j: (0, j)),
                  pl.BlockSpec((D_MODEL, tf), lambda i, j: (0, nf + j)),
                  pl.BlockSpec((9, tf), lambda i, j: (0, j)),
                  pl.BlockSpec((1, tf), lambda i, j: (0, j)),
                  pl.BlockSpec((tf, D_MODEL), lambda i, j: (j, 0)),
                  full(prm['ln2_g']), full(prm['ln2_b'])],
        out_specs=pl.BlockSpec((tm, D_MODEL), lambda i, j: (i, 0)),
        out_shape=jax.ShapeDtypeStruct((tokens, D_MODEL), F32),
        scratch_shapes=[pltpu.VMEM((tm, D_MODEL), BF16), pltpu.VMEM((tm, D_MODEL), F32),
                        pltpu.VMEM((4, tm, LANES), F32)],
        compiler_params=_cparams(("arbitrary", "arbitrary")),
        name="conv_ffn",
    )(x2, mod, prm['w_ffn_up'], prm['w_ffn_up'], prm['ffn_conv_w'], prm['ffn_conv_b'], prm['w_ffn_down'],
      prm['ln2_g'], prm['ln2_b'])


def _layer(x, mod, s0, on_grid, prm):
    batch, seq, _ = x.shape
    tokens = batch * seq
    p2 = _inproj(x.reshape(tokens, D_MODEL), mod, prm['w_in'], prm['conv_in'], seq)
    p3 = p2.reshape(batch, seq, C_IN)

    spec = _hyena_filter_spectrum(seq, prm['filt_w1'], prm['filt_b1'], prm['filt_w2'], prm['filt_b2'],
                                  prm['filt_w3'], prm['filt_freq'])
    yh = _hyena(p3, spec, prm['hyena_bias'], seq)

    shared, directed, bonus, g = _rwkv_prep(p3, prm)
    half = batch * HEADS
    flat = lambda a: a.reshape(a.shape[:-4] + (half, seq // 2, LANES))
    if s0 is None:
        s0l = jnp.zeros((HEAD, HEAD, 2 * half), F32)
    else:
        s0l = jnp.transpose(s0, (3, 4, 1, 0, 2)).reshape(HEAD, HEAD, 2 * half)
    yf, yb, off_f, off_b, st = _scan(flat(shared), flat(directed), s0l)
    s_final = jnp.transpose(st.reshape(HEAD, HEAD, 2, batch, HEADS), (3, 2, 4, 0, 1))

    x1 = _mix(yf, yb, off_f, off_b, bonus, g, yh, x, mod, prm)
    out = _ffn(x1.reshape(tokens, D_MODEL), mod, seq, on_grid, prm)
    return out.reshape(batch, seq, D_MODEL), s_final


def kernel(x_prompt, x_sample, state_rwkv, c, c_ctx, ada_w, ada_b, w_in, conv_in, filt_w1, filt_b1, filt_w2, filt_b2, filt_w3, filt_freq, hyena_bias, rwkv_w0, rwkv_w_lora, rwkv_a0, rwkv_a_lora, rwkv_g_lora, rwkv_k_k, rwkv_k_a, rwkv_r_k, lnx_g, lnx_b, w_out, ln1_g, ln1_b, w_ffn_up, ffn_conv_w, ffn_conv_b, w_ffn_down, ln2_g, ln2_b):
    dec_batch = x_sample.shape[0]
    ones = jnp.asarray(_head_ones())
    y_prompt, y_sample = x_prompt, x_sample
    ctx_states = []
    for layer in range(DEPTH):
        row = lambda a: a[layer][None]
        prm = dict(
            ones=ones, w_in=w_in[layer].astype(BF16), conv_in=conv_in[layer],
            filt_w1=filt_w1[layer], filt_b1=filt_b1[layer], filt_w2=filt_w2[layer], filt_b2=filt_b2[layer],
            filt_w3=filt_w3[layer], filt_freq=filt_freq[layer], hyena_bias=row(hyena_bias),
            rwkv_w0=rwkv_w0[layer], rwkv_w_lora=rwkv_w_lora[layer], rwkv_a0=rwkv_a0[layer],
            rwkv_a_lora=rwkv_a_lora[layer], rwkv_g_lora=rwkv_g_lora[layer], rwkv_k_k=row(rwkv_k_k),
            rwkv_k_a=row(rwkv_k_a), rwkv_r_k=rwkv_r_k[layer].reshape(1, D_RWKV),
            lnx_g=row(lnx_g), lnx_b=row(lnx_b), w_out=w_out[layer].astype(BF16),
            ln1_g=row(ln1_g), ln1_b=row(ln1_b), w_ffn_up=w_ffn_up[layer].astype(BF16),
            ffn_conv_w=ffn_conv_w[layer].reshape(9, D_FF), ffn_conv_b=row(ffn_conv_b),
            w_ffn_down=(0.5 * w_ffn_down[layer]).astype(BF16),
            ln2_g=row(ln2_g), ln2_b=row(ln2_b))
        pad = (-(dec_batch + 1)) % SUBLANES
        cond = jnp.concatenate([c, c_ctx[None], jnp.zeros((pad, D_MODEL), F32)], axis=0)
        mod = _ada(cond, ada_w[layer], row(ada_b)).reshape(cond.shape[0], 6, D_MODEL)
        y_prompt, s_ctx = _layer(y_prompt, mod[dec_batch:dec_batch + 1], None, False, prm)
        ctx_states.append(s_ctx)
        y_sample, _ = _layer(y_sample, mod[:dec_batch], state_rwkv[:, layer], True, prm)
    return (y_prompt, y_sample, jnp.stack(ctx_states, axis=1))
```

```python
import functools
import math

import ml_dtypes
import numpy as np
import jax
import jax.numpy as jnp
from jax import lax
from jax.experimental import pallas as pl
from jax.experimental.pallas import tpu as pltpu

F32 = jnp.float32
BF16 = jnp.bfloat16

D_MODEL = 1024
D_HYENA = 512
D_RWKV = 512
HEAD = 64
HEADS = D_RWKV // HEAD
LORA_W = 64
LORA_A = 64
LORA_G = 128
LORA_ALL = LORA_W + LORA_A + LORA_G
C_IN = 3 * D_HYENA + 3 * D_RWKV + LORA_ALL
FILT_BANDS = 16
FILT_FEAT = 1 + 2 * FILT_BANDS
FILT_HIDDEN = 64
N_FILT = 2 * D_HYENA
HYENA_TARGET = 1e-2
HYENA_FAST_PCT = 0.3
HYENA_SLOW_PCT = 1.5
D_FF = 2816
GRID_W = 64
DEPTH = 1
ALPHA = (2.0 * DEPTH) ** 0.25
LN_EPS = 1e-5
GN_EPS = 64e-5
NORM_EPS = 1e-12

LANES = 128
SUBLANES = 8
MXU_N = 256
ROW_TILE = 1024
SCAN_STEPS = 32
VMEM_LIMIT = 56 * 1024 * 1024


def _cparams(sem):
    return pltpu.CompilerParams(dimension_semantics=sem, vmem_limit_bytes=VMEM_LIMIT)


def _dot(a, b):
    return jnp.dot(a, b, preferred_element_type=F32)


def _split(x):
    hi = x.astype(BF16)
    lo = (x - hi.astype(F32)).astype(BF16)
    return hi, lo


def _dot_hi_const(x, c):
    hi, lo = _split(x)
    return _dot(hi, c) + _dot(lo, c)


def _dot3(ah, al, bh, bl):
    return _dot(ah, bh) + _dot(ah, bl) + _dot(al, bh)


def _layernorm_rows(x):
    mu = jnp.mean(x, axis=-1, keepdims=True)
    xc = x - mu
    var = jnp.mean(xc * xc, axis=-1, keepdims=True)
    return xc * lax.rsqrt(var + LN_EPS)


def _row_tile(tokens, seq):
    for tm in (ROW_TILE, 512, 256):
        if tokens % tm == 0 and (tm % seq == 0 or seq % tm == 0):
            return tm
    raise ValueError(f"no row tile for {tokens} tokens of sequence length {seq}")


def _np_split(x64):
    hi = x64.astype(ml_dtypes.bfloat16)
    lo = (x64 - hi.astype(np.float64)).astype(ml_dtypes.bfloat16)
    return hi, lo


@functools.lru_cache(maxsize=None)
def _dft_constants(L):
    N = 2 * L
    f = np.arange(L, dtype=np.int64)[:, None]
    t = np.arange(L, dtype=np.int64)[None, :]
    ang = (2.0 * np.pi / N) * ((f * t) % N).astype(np.float64)
    cos, sin = np.cos(ang), np.sin(ang)
    nyq = np.where(np.arange(L) % 2 == 0, 1.0, -1.0)
    fwd_s = sin.copy()
    fwd_s[0, :] = nyq
    fwd = np.concatenate([cos, fwd_s], axis=0)
    wf = np.full((L, 1), 2.0)
    wf[0, 0] = 1.0
    inv_c = (wf * cos).T / N
    inv_s = (2.0 * sin).T / N
    inv_s[:, 0] = nyq / N
    inv = np.concatenate([inv_c, inv_s], axis=1)
    return _np_split(fwd) + _np_split(inv)


@functools.lru_cache(maxsize=None)
def _filter_constants(L):
    t = np.arange(L, dtype=np.float64)[:, None] / L
    bands = np.arange(1, FILT_BANDS + 1, dtype=np.float64)[None, :]
    ang = (2.0 * math.pi) * bands * t
    feat = np.concatenate([t, np.sin(ang), np.cos(ang)], axis=-1)
    feat = np.pad(feat, ((0, 0), (0, FILT_HIDDEN - FILT_FEAT)))
    slow = abs(math.log(HYENA_TARGET) / HYENA_SLOW_PCT)
    fast = abs(math.log(HYENA_TARGET) / HYENA_FAST_PCT)
    deltas = np.linspace(slow, fast, N_FILT, dtype=np.float64)
    window = np.exp(-t * deltas[None, :])
    return feat.astype(np.float32), window.astype(np.float32)


@functools.lru_cache(maxsize=None)
def _head_ones():
    h = np.arange(D_RWKV) // HEAD
    return (h[:, None] == h[None, :]).astype(ml_dtypes.bfloat16)


def _ada_kernel(c_ref, w_ref, b_ref, o_ref):
    c = c_ref[...]
    s = c * jax.nn.sigmoid(c)
    o_ref[...] = _dot(s.astype(BF16), w_ref[...].astype(BF16)) + b_ref[...]


def _ada(cond, ada_w, ada_b):
    rows = cond.shape[0]
    tn = 1024
    return pl.pallas_call(
        _ada_kernel,
        grid=(6 * D_MODEL // tn,),
        in_specs=[pl.BlockSpec((rows, D_MODEL), lambda j: (0, 0)),
                  pl.BlockSpec((D_MODEL, tn), lambda j: (0, j)),
                  pl.BlockSpec((1, tn), lambda j: (0, j))],
        out_specs=pl.BlockSpec((rows, tn), lambda j: (0, j)),
        out_shape=jax.ShapeDtypeStruct((rows, 6 * D_MODEL), F32),
        compiler_params=_cparams(("arbitrary",)),
        name="ada_mod",
    )(cond, ada_w, ada_b)


def _inproj_kernel(x_ref, mod_ref, w_ref, cw_ref, o_ref, h_ref, mask_ref, *, seq):
    @pl.when(pl.program_id(1) == 0)
    def _():
        m = mod_ref[0]
        h = _layernorm_rows(x_ref[...]) * (1.0 + m[1:2]) + m[0:1]
        h_ref[...] = h.astype(BF16)
        t = lax.broadcasted_iota(jnp.int32, mask_ref.shape[1:], 0) & (seq - 1)
        mask_ref[0] = (t != 0).astype(F32)
        mask_ref[1] = (t != seq - 1).astype(F32)

    p = _dot(h_ref[...], w_ref[...])
    rows = p.shape[0]
    wide = lambda k: jnp.concatenate([mask_ref[k]] * (p.shape[1] // LANES), axis=1)
    prev = pltpu.roll(p, 1, 0) * wide(0)
    nxt = pltpu.roll(p, rows - 1, 0) * wide(1)
    cw = cw_ref[...]
    o_ref[...] = cw[0:1] * prev + cw[1:2] * p + cw[2:3] * nxt


def _inproj(x2, mod, w_in_bf, conv_in, seq):
    tokens = x2.shape[0]
    tm = _row_tile(tokens, seq)
    nmod = mod.shape[0]
    assert nmod == 1 or tm <= seq
    tn = MXU_N
    return pl.pallas_call(
        functools.partial(_inproj_kernel, seq=seq),
        grid=(tokens // tm, C_IN // tn),
        in_specs=[pl.BlockSpec((tm, D_MODEL), lambda i, j: (i, 0)),
                  pl.BlockSpec((1, 6, D_MODEL), lambda i, j: ((i * tm // seq) % nmod, 0, 0)),
                  pl.BlockSpec((D_MODEL, tn), lambda i, j: (0, j)),
                  pl.BlockSpec((3, tn), lambda i, j: (0, j))],
        out_specs=pl.BlockSpec((tm, tn), lambda i, j: (i, j)),
        out_shape=jax.ShapeDtypeStruct((tokens, C_IN), F32),
        scratch_shapes=[pltpu.VMEM((tm, D_MODEL), BF16), pltpu.VMEM((2, tm, LANES), F32)],
        compiler_params=_cparams(("arbitrary", "arbitrary")),
        name="inproj_conv",
    )(x2, mod, w_in_bf, conv_in)


def _filter_kernel(feat_ref, win_ref, w1_ref, b1_ref, w2_ref, b2_ref, w3_ref, fr_ref,
                   fh_ref, fl_ref, a_ref, an_ref, b_ref, *, seq):
    def hp_dot(x, w):
        xh, xl = _split(x)
        wh, wl = _split(w)
        return _dot3(xh, xl, wh, wl)

    fr = fr_ref[...]
    h = jnp.sin(fr[0:1] * (hp_dot(feat_ref[...], w1_ref[...]) + b1_ref[...]))
    h = jnp.sin(fr[1:2] * (hp_dot(h, w2_ref[...]) + b2_ref[...]))
    h = hp_dot(h, w3_ref[...]) * win_ref[...]
    col = jnp.sum(jnp.abs(h), axis=0, keepdims=True)
    inv = 1.0 / (col[:, :D_HYENA] + col[:, D_HYENA:])
    hf = h[:, :D_HYENA] * inv
    hb = h[:, D_HYENA:] * inv
    row = lax.broadcasted_iota(jnp.int32, hf.shape, 0)
    hb = jnp.where(row == 0, 0.0, hb)
    eh, el = _split(hf + hb)
    oh, ol = _split(hf - hb)
    fh, fl = fh_ref[...], fl_ref[...]
    fe = _dot3(fh, fl, eh, el)
    fo = _dot3(fh[seq:], fl[seq:], oh, ol)
    a = fe[:seq]
    a_ref[...] = a
    an_ref[...] = jnp.where(row == 0, fe[seq:seq + 1], a)
    b_ref[...] = jnp.where(row == 0, 0.0, fo)


def _hyena_filter_spectrum(seq, w1, b1, w2, b2, w3, freq):
    feat, window = _filter_constants(seq)
    fh, fl, _, _ = _dft_constants(seq)
    w1p = jnp.pad(w1, ((0, FILT_HIDDEN - FILT_FEAT), (0, 0)))
    shp = jax.ShapeDtypeStruct((seq, D_HYENA), F32)
    return pl.pallas_call(
        functools.partial(_filter_kernel, seq=seq),
        out_shape=(shp, shp, shp),
        compiler_params=pltpu.CompilerParams(vmem_limit_bytes=VMEM_LIMIT),
        name="hyena_filter",
    )(feat, window, w1p, b1[None], w2, b2[None], w3, freq, fh, fl)


def _hyena_kernel(hv_ref, hx0_ref, hx1_ref, fh_ref, fl_ref, gh_ref, gl_ref, a_ref, an_ref, b_ref,
                  bias_ref, o_ref, *, seq):
    z = hx1_ref[...] * hv_ref[...]
    zb = z.astype(BF16)
    zf = _dot(fh_ref[...], zb) + _dot(fl_ref[...], zb)
    zc, zs = zf[:seq], zf[seq:]
    a, an, b = a_ref[...], an_ref[...], b_ref[...]
    yc = zc * a - zs * b
    ys = zc * b + zs * an
    yb = jnp.concatenate([yc, ys], axis=0).astype(BF16)
    y = _dot(gh_ref[...], yb) + _dot(gl_ref[...], yb)
    o_ref[...] = (hx0_ref[...] * (y + bias_ref[...] * z)).astype(o_ref.dtype)


def _hyena(p3, spec, bias, seq):
    batch = p3.shape[0]
    fh, fl, gh, gl = _dft_constants(seq)
    a, an, b = spec
    cb = MXU_N
    ncb = D_HYENA // cb
    const = lambda shape: pl.BlockSpec(shape, lambda c, i: (0, 0), pipeline_mode=pl.Buffered(1))
    col = lambda k: pl.BlockSpec((None, seq, cb), lambda c, i: (i, 0, k * ncb + c))
    chan = lambda rows: pl.BlockSpec((rows, cb), lambda c, i: (0, c))
    return pl.pallas_call(
        functools.partial(_hyena_kernel, seq=seq),
        grid=(ncb, batch),
        in_specs=[col(0), col(1), col(2),
                  const((2 * seq, seq)), const((2 * seq, seq)), const((seq, 2 * seq)), const((seq, 2 * seq)),
                  chan(seq), chan(seq), chan(seq), chan(1)],
        out_specs=pl.BlockSpec((None, seq, cb), lambda c, i: (i, 0, c)),
        out_shape=jax.ShapeDtypeStruct((batch, seq, D_HYENA), BF16),
        compiler_params=_cparams(("arbitrary", "arbitrary")),
        name="hyena_conv",
    )(p3, p3, p3, fh, fl, gh, gl, a, an, b, bias)


PAIR_TILE = 256


def _store_paired(x, xs_ref, put):
    rows = x.shape[0]
    for hp in range(HEADS // 2):
        xs_ref[hp] = x[:, hp * LANES:(hp + 1) * LANES]
    low = lax.broadcasted_iota(jnp.int32, (rows // 2, LANES), 1) < HEAD
    for hp in range(HEADS // 2):
        even = xs_ref[hp, pl.ds(0, rows // 2, stride=2), :]
        odd = xs_ref[hp, pl.ds(1, rows // 2, stride=2), :]
        put(2 * hp, jnp.where(low, even, pltpu.roll(odd, HEAD, 1)))
        put(2 * hp + 1, jnp.where(low, pltpu.roll(even, HEAD, 1), odd))


def _load_paired(get, xs_ref):
    rows = xs_ref.shape[1]
    low = lax.broadcasted_iota(jnp.int32, (rows // 2, LANES), 1) < HEAD
    for hp in range(HEADS // 2):
        b0, b1 = get(2 * hp), get(2 * hp + 1)
        xs_ref[hp, pl.ds(0, rows // 2, stride=2), :] = jnp.where(low, b0, pltpu.roll(b1, HEAD, 1))
        xs_ref[hp, pl.ds(1, rows // 2, stride=2), :] = jnp.where(low, pltpu.roll(b0, HEAD, 1), b1)
    return jnp.concatenate([xs_ref[hp] for hp in range(HEADS // 2)], axis=1)


def _rwkv_prep_kernel(r_ref, k_ref, v_ref, lora_ref, ones_ref, w0_ref, wl_ref, a0_ref, al_ref, gl_ref,
                      kk_ref, ka_ref, rk_ref,
                      sh_ref, dr_ref, bonus_ref, g_ref, xs_ref):
    r, k, v = r_ref[...], k_ref[...], v_ref[...]
    lora = lora_ref[...]
    wd = lora[:, :LORA_W]
    ad = lora[:, LORA_W:LORA_W + LORA_A]
    gd = lora[:, LORA_W + LORA_A:]
    ones = ones_ref[...]

    def shared(q):
        def put(h, val):
            sh_ref[q, h] = val
        return put

    def directed(q, d):
        def put(h, val):
            dr_ref[q, d, h] = val
        return put

    g_ref[...] = _dot(jax.nn.sigmoid(gd).astype(BF16), gl_ref[...].astype(BF16))
    kk = k * kk_ref[...]
    norm = jnp.sqrt(_dot_hi_const(kk * kk, ones))
    kk = kk / jnp.maximum(norm, NORM_EPS)
    _store_paired(r, xs_ref, shared(0))
    _store_paired(v, xs_ref, shared(1))
    _store_paired(-kk, xs_ref, shared(2))
    tw = jnp.tanh(wd).astype(BF16)
    adb = ad.astype(BF16)
    ksum = jnp.zeros_like(k)
    for d in range(2):
        w_raw = w0_ref[d:d + 1] + _dot(tw, wl_ref[d].astype(BF16))
        _store_paired(jnp.exp(-math.exp(-0.5) * jax.nn.sigmoid(w_raw)), xs_ref, directed(0, d))
        a = jax.nn.sigmoid(a0_ref[d:d + 1] + _dot(adb, al_ref[d].astype(BF16)))
        kd = k * (1.0 + (a - 1.0) * ka_ref[...])
        _store_paired(kd, xs_ref, directed(1, d))
        _store_paired(kk * a, xs_ref, directed(2, d))
        ksum = ksum + kd
    bonus_ref[...] = _dot_hi_const(r * ksum * rk_ref[...], ones) * v


def _rwkv_prep(p3, prm):
    batch, seq, _ = p3.shape
    tm = PAIR_TILE
    cbase = 3 * D_HYENA // D_RWKV
    col = lambda k: pl.BlockSpec((None, tm, D_RWKV), lambda b, i: (b, i, cbase + k))
    full = lambda a: pl.BlockSpec(a.shape, lambda b, i: (0,) * a.ndim)
    params = [prm['ones'], prm['rwkv_w0'], prm['rwkv_w_lora'], prm['rwkv_a0'], prm['rwkv_a_lora'],
              prm['rwkv_g_lora'], prm['rwkv_k_k'], prm['rwkv_k_a'], prm['rwkv_r_k']]
    shared = jax.ShapeDtypeStruct((3, batch, HEADS, seq // 2, LANES), F32)
    directed = jax.ShapeDtypeStruct((3, 2, batch, HEADS, seq // 2, LANES), F32)
    rows = jax.ShapeDtypeStruct((batch, seq, D_RWKV), F32)
    shared_spec = pl.BlockSpec((3, None, HEADS, tm // 2, LANES), lambda b, i: (0, b, 0, i, 0))
    directed_spec = pl.BlockSpec((3, 2, None, HEADS, tm // 2, LANES), lambda b, i: (0, 0, b, 0, i, 0))
    rows_spec = pl.BlockSpec((None, tm, D_RWKV), lambda b, i: (b, i, 0))
    return pl.pallas_call(
        _rwkv_prep_kernel,
        grid=(batch, seq // tm),
        in_specs=[col(0), col(1), col(2),
                  pl.BlockSpec((None, tm, LORA_ALL), lambda b, i: (b, i, (C_IN - LORA_ALL) // LORA_ALL))]
                 + [full(a) for a in params],
        out_specs=[shared_spec, directed_spec, rows_spec, rows_spec],
        out_shape=(shared, directed, rows, rows),
        scratch_shapes=[pltpu.VMEM((HEADS // 2, tm, LANES), F32)],
        compiler_params=_cparams(("arbitrary", "arbitrary")),
        name="rwkv_prep",
    )(p3, p3, p3, p3, *params)


V_BLOCK = 8
N_SCAN_OPERANDS = 6
_R, _VV, _A, _W, _K, _B = range(N_SCAN_OPERANDS)


def _scan_kernel(*refs, mixed, m_tiles):
    if mixed:
        sf_ref, df_ref, sb_ref, db_ref, s0_ref, yf_ref, yb_ref, st_ref = refs[:8]
    else:
        sf_ref, df_ref, s0_ref, y_ref, st_ref = refs[:5]
    s_ref, in_a, in_b, yo_a, yo_b = refs[-5:]
    npair = SCAN_STEPS // 2
    tc = pl.program_id(1)
    backward = pl.program_id(0) >= m_tiles

    @pl.when(tc == 0)
    def _():
        s_ref[...] = s0_ref[...]

    low = lax.broadcasted_iota(jnp.int32, (HEAD, LANES), 1) < HEAD

    def pair_rows(ref, qq, j):
        scans = ref.shape[1]
        flat = ref.at[qq].reshape(scans * npair, LANES)
        return flat[pl.ds(j, scans, stride=npair), :]

    def stage(dst, jj):
        src_b = npair - 1 - jj
        src = jnp.where(backward, src_b, jj)
        for q in range(N_SCAN_OPERANDS):
            f_ref, b_ref = (sf_ref, sb_ref if mixed else None) if q < 3 else (df_ref, db_ref if mixed else None)
            qq = q % 3
            if mixed:
                t = jnp.concatenate([pair_rows(f_ref, qq, jj), pair_rows(b_ref, qq, src_b)], axis=0).T
                dst[0, q] = jnp.where(low, t[:HEAD], t[HEAD:])
                dst[1, q] = jnp.where(low, t[HEAD:], t[:HEAD])
            else:
                t = pair_rows(f_ref, qq, src).T
                dst[0, q] = jnp.where(backward, t[HEAD:], t[:HEAD])
                dst[1, q] = jnp.where(backward, t[:HEAD], t[HEAD:])

    sub = lax.broadcasted_iota(jnp.int32, (SUBLANES, LANES), 0)

    def block_sum(x):
        acc = x[0:SUBLANES]
        for j in range(1, HEAD // SUBLANES):
            acc = acc + x[j * SUBLANES:(j + 1) * SUBLANES]
        return acc

    def all_sum(p):
        p = p + pltpu.roll(p, 4, 0)
        p = p + pltpu.roll(p, 2, 0)
        return p + pltpu.roll(p, 1, 0)

    def fold(x, y, sh):
        m = (sub & sh) == 0
        return jnp.where(m, x, pltpu.roll(y, sh, 0)) + jnp.where(m, pltpu.roll(x, SUBLANES - sh, 0), y)

    def rows_sum(ps):
        q = [ps[j] for j in (0, 4, 2, 6, 1, 5, 3, 7)]
        s4 = [fold(q[2 * i], q[2 * i + 1], 4) for i in range(4)]
        return fold(fold(s4[0], s4[1], 2), fold(s4[2], s4[3], 2), 1)

    def compute(src, yo):
        for sidx in range(2):
            r, w, kk = src[sidx, _R], src[sidx, _W], src[sidx, _K]
            a, b = src[sidx, _A], src[sidx, _B]
            for vb in range(HEAD // V_BLOCK):
                base = vb * V_BLOCK
                sas = [all_sum(block_sum(s_ref[base + i] * a)) for i in range(V_BLOCK)]
                ps = []
                for i in range(V_BLOCK):
                    vi = base + i
                    vrow = src[sidx, _VV, vi:vi + 1, :]
                    s_ref[vi] = s_ref[vi] * w + jnp.tile(sas[i], (HEAD // SUBLANES, 1)) * b + vrow * kk
                    ps.append(block_sum(s_ref[vi] * r))
                for g in range(V_BLOCK // SUBLANES):
                    lo = sidx * HEAD + base + g * SUBLANES
                    yo[lo:lo + SUBLANES, :] = rows_sum(ps[g * SUBLANES:(g + 1) * SUBLANES])

    def emit(yo, jj):
        src_b = npair - 1 - jj
        z = yo[...].T
        if mixed:
            yf_ref[:, jj, :] = z[:HEAD]
            yb_ref[:, src_b, :] = pltpu.roll(z[HEAD:], HEAD, 1)
        else:
            y_ref[:, jnp.where(backward, src_b, jj), :] = jnp.where(backward, pltpu.roll(z, HEAD, 1), z)

    stage(in_a, 0)

    def two_pairs(i, carry):
        j0 = 2 * i
        stage(in_b, j0 + 1)
        compute(in_a, yo_a)
        stage(in_a, jnp.minimum(j0 + 2, npair - 1))
        emit(yo_a, j0)
        compute(in_b, yo_b)
        emit(yo_b, j0 + 1)
        return carry

    lax.fori_loop(0, npair // 2, two_pairs, 0)

    @pl.when(tc == pl.num_programs(1) - 1)
    def _():
        st_ref[...] = s_ref[...]


def _scan(shared, directed, s0):
    _, half, hseq, _ = shared.shape
    npair = SCAN_STEPS // 2
    n = hseq // npair
    mixed = half < LANES
    st = pl.BlockSpec((HEAD, HEAD, LANES), lambda l, t: (0, 0, l))
    stage_buf = pltpu.VMEM((2, N_SCAN_OPERANDS, HEAD, LANES), F32)
    y_buf = pltpu.VMEM((2 * HEAD, LANES), F32)
    scratch = [pltpu.VMEM((HEAD, HEAD, LANES), F32), stage_buf, stage_buf, y_buf, y_buf]
    st_shape = jax.ShapeDtypeStruct((HEAD, HEAD, 2 * half), F32)
    if mixed:
        assert 2 * half == LANES
        blk = (half, npair, LANES)
        sf = pl.BlockSpec((3,) + blk, lambda l, t: (0, 0, t, 0))
        sb = pl.BlockSpec((3,) + blk, lambda l, t: (0, 0, n - 1 - t, 0))
        df = pl.BlockSpec((3, None) + blk, lambda l, t: (0, 0, 0, t, 0))
        db = pl.BlockSpec((3, None) + blk, lambda l, t: (0, 1, 0, n - 1 - t, 0))
        yf_spec = pl.BlockSpec(blk, lambda l, t: (0, t, 0))
        yb_spec = pl.BlockSpec(blk, lambda l, t: (0, n - 1 - t, 0))
        yshape = jax.ShapeDtypeStruct((half, hseq, LANES), F32)
        yf, yb, stt = pl.pallas_call(
            functools.partial(_scan_kernel, mixed=True, m_tiles=1),
            grid=(1, n),
            in_specs=[sf, df, sb, db, st],
            out_specs=[yf_spec, yb_spec, st],
            out_shape=(yshape, yshape, st_shape),
            scratch_shapes=scratch,
            compiler_params=_cparams(("arbitrary", "arbitrary")),
            name="rwkv_scan_mixed",
        )(shared, directed, shared, directed, s0)
        return yf, yb, 0, 0, stt
    assert half % LANES == 0
    m = half // LANES
    chunk = lambda l, t: jnp.where(l >= m, n - 1 - t, t)
    blk = (LANES, npair, LANES)
    sp = pl.BlockSpec((3,) + blk, lambda l, t: (0, l % m, chunk(l, t), 0))
    dp = pl.BlockSpec((3, None) + blk, lambda l, t: (0, l // m, l % m, chunk(l, t), 0))
    yp = pl.BlockSpec((None,) + blk, lambda l, t: (l // m, l % m, chunk(l, t), 0))
    y, stt = pl.pallas_call(
        functools.partial(_scan_kernel, mixed=False, m_tiles=m),
        grid=(2 * m, n),
        in_specs=[sp, dp, st],
        out_specs=[yp, st],
        out_shape=(jax.ShapeDtypeStruct((2, half, hseq, LANES), F32), st_shape),
        scratch_shapes=scratch,
        compiler_params=_cparams(("arbitrary", "arbitrary")),
        name="rwkv_scan",
    )(shared, directed, s0)
    y2 = y.reshape(2 * half, hseq, LANES)
    return y2, y2, 0, half, stt


def _mix_kernel(yf_ref, yb_ref, bonus_ref, g_ref, yh_ref, x_ref, mod_ref, ones_ref, wo_ref, lng_ref, lnb_ref,
                l1g_ref, l1b_ref, o_ref, xs_ref):
    ones = ones_ref[...]
    y = _load_paired(lambda h: yf_ref[h] + yb_ref[h], xs_ref)
    mu = _dot_hi_const(y, ones) * (1.0 / HEAD)
    yc = y - mu
    var = _dot_hi_const(yc * yc, ones) * (1.0 / HEAD)
    yr = yc * lax.rsqrt(var + GN_EPS) * lng_ref[...] + lnb_ref[...] + bonus_ref[...]
    yr = (yr * g_ref[...]).astype(BF16)
    mix = _dot(yh_ref[...], wo_ref[:D_HYENA, :]) + _dot(yr, wo_ref[D_HYENA:, :])
    m = mod_ref[0]
    z = _layernorm_rows(ALPHA * x_ref[...] + m[2:3] * mix)
    o_ref[...] = z * l1g_ref[...] + l1b_ref[...]


def _mix(yf, yb, off_f, off_b, bonus, g, yh, x, mod, prm):
    batch, seq, _ = x.shape
    tm = PAIR_TILE
    nmod = mod.shape[0]
    yf4 = yf.reshape(-1, HEADS, seq // 2, LANES)
    yb4 = yb.reshape(-1, HEADS, seq // 2, LANES)
    bf, bb = off_f // HEADS, off_b // HEADS
    row = lambda width: pl.BlockSpec((None, tm, width), lambda b, i: (b, i, 0))
    full = lambda a: pl.BlockSpec(a.shape, lambda b, i: (0,) * a.ndim)
    params = [prm['ones'], prm['w_out'], prm['lnx_g'], prm['lnx_b'], prm['ln1_g'], prm['ln1_b']]
    return pl.pallas_call(
        _mix_kernel,
        grid=(batch, seq // tm),
        in_specs=[pl.BlockSpec((None, HEADS, tm // 2, LANES), lambda b, i: (bf + b, 0, i, 0)),
                  pl.BlockSpec((None, HEADS, tm // 2, LANES), lambda b, i: (bb + b, 0, i, 0)),
                  row(D_RWKV), row(D_RWKV), row(D_HYENA), row(D_MODEL),
                  pl.BlockSpec((1, 6, D_MODEL), lambda b, i: (b % nmod, 0, 0))]
                 + [full(a) for a in params],
        out_specs=row(D_MODEL),
        out_shape=jax.ShapeDtypeStruct((batch, seq, D_MODEL), F32),
        scratch_shapes=[pltpu.VMEM((HEADS // 2, tm, LANES), F32)],
        compiler_params=_cparams(("arbitrary", "arbitrary")),
        name="mix_outproj",
    )(yf4, yb4, bonus, g, yh, x, mod, *params)


def _ffn_kernel(x_ref, mod_ref, wu_ref, wg_ref, cw_ref, cb_ref, wd_ref, l2g_ref, l2b_ref, o_ref,
                h_ref, acc_ref, mask_ref, *, seq, on_grid):
    j = pl.program_id(1)
    width = GRID_W if on_grid else seq

    @pl.when(j == 0)
    def _():
        m = mod_ref[0]
        h = _layernorm_rows(x_ref[...]) * (1.0 + m[4:5]) + m[3:4]
        h_ref[...] = h.astype(BF16)
        acc_ref[...] = jnp.zeros_like(acc_ref)
        t = lax.broadcasted_iota(jnp.int32, mask_ref.shape[1:], 0)
        col = t & (width - 1)
        grow = (t & (seq - 1)) >> int(math.log2(GRID_W))
        mask_ref[0] = (col != 0).astype(F32)
        mask_ref[1] = (col != width - 1).astype(F32)
        mask_ref[2] = (grow != 0).astype(F32)
        mask_ref[3] = (grow != seq // GRID_W - 1).astype(F32)

    h = h_ref[...]
    u = _dot(h, wu_ref[...])
    rows = u.shape[0]
    cw = cw_ref[...]
    wide = lambda k: jnp.concatenate([mask_ref[k]] * (u.shape[1] // LANES), axis=1)
    um = pltpu.roll(u, 1, 0) * wide(0)
    up = pltpu.roll(u, rows - 1, 0) * wide(1)
    if on_grid:
        hrow = [cw[3 * di:3 * di + 1] * um + cw[3 * di + 1:3 * di + 2] * u + cw[3 * di + 2:3 * di + 3] * up
                for di in range(3)]
        if rows == seq:
            pad = jnp.zeros((GRID_W, u.shape[1]), F32)
            u = (hrow[1] + jnp.concatenate([pad, hrow[0][:rows - GRID_W]], axis=0)
                 + jnp.concatenate([hrow[2][GRID_W:], pad], axis=0))
        else:
            u = (hrow[1] + pltpu.roll(hrow[0], GRID_W, 0) * wide(2)
                 + pltpu.roll(hrow[2], rows - GRID_W, 0) * wide(3))
    else:
        u = cw[3:4] * um + cw[4:5] * u + cw[5:6] * up
    u = u + cb_ref[...]
    c0 = math.sqrt(2.0 / math.pi)
    act = u + u * jnp.tanh(u * (c0 + (c0 * 0.044715) * (u * u)))
    gt = _dot(h, wg_ref[...])
    acc_ref[...] += _dot((act * gt).astype(BF16), wd_ref[...])

    @pl.when(j == pl.num_programs(1) - 1)
    def _():
        m = mod_ref[0]
        z = _layernorm_rows(ALPHA * x_ref[...] + m[5:6] * acc_ref[...])
        o_ref[...] = z * l2g_ref[...] + l2b_ref[...]


def _ffn(x2, mod, seq, on_grid, prm):
    tokens = x2.shape[0]
    tm = _row_tile(tokens, seq)
    nmod = mod.shape[0]
    assert nmod == 1 or tm <= seq
    assert not on_grid or tm % seq == 0
    tf = MXU_N
    nf = D_FF // tf
    full = lambda a: pl.BlockSpec(a.shape, lambda i, j: (0,) * a.ndim)
    return pl.pallas_call(
        functools.partial(_ffn_kernel, seq=seq, on_grid=on_grid),
        grid=(tokens // tm, nf),
        in_specs=[pl.BlockSpec((tm, D_MODEL), lambda i, j: (i, 0)),
                  pl.BlockSpec((1, 6, D_MODEL), lambda i, j: ((i * tm // seq) % nmod, 0, 0)),
                  pl.BlockSpec((D_MODEL, tf), lambda i, j: (0, j)),
                  pl.BlockSpec((D_MODEL, tf), lambda i, j: (0, nf + j)),
                  pl.BlockSpec((9, tf), lambda i, j: (0, j)),
                  pl.BlockSpec((1, tf), lambda i, j: (0, j)),
                  pl.BlockSpec((tf, D_MODEL), lambda i, j: (j, 0)),
                  full(prm['ln2_g']), full(prm['ln2_b'])],
        out_specs=pl.BlockSpec((tm, D_MODEL), lambda i, j: (i, 0)),
        out_shape=jax.ShapeDtypeStruct((tokens, D_MODEL), F32),
        scratch_shapes=[pltpu.VMEM((tm, D_MODEL), BF16), pltpu.VMEM((tm, D_MODEL), F32),
                        pltpu.VMEM((4, tm, LANES), F32)],
        compiler_params=_cparams(("arbitrary", "arbitrary")),
        name="conv_ffn",
    )(x2, mod, prm['w_ffn_up'], prm['w_ffn_up'], prm['ffn_conv_w'], prm['ffn_conv_b'], prm['w_ffn_down'],
      prm['ln2_g'], prm['ln2_b'])


def _layer(x, mod, s0, on_grid, prm):
    batch, seq, _ = x.shape
    tokens = batch * seq
    p2 = _inproj(x.reshape(tokens, D_MODEL), mod, prm['w_in'], prm['conv_in'], seq)
    p3 = p2.reshape(batch, seq, C_IN)

    spec = _hyena_filter_spectrum(seq, prm['filt_w1'], prm['filt_b1'], prm['filt_w2'], prm['filt_b2'],
                                  prm['filt_w3'], prm['filt_freq'])
    yh = _hyena(p3, spec, prm['hyena_bias'], seq)

    shared, directed, bonus, g = _rwkv_prep(p3, prm)
    half = batch * HEADS
    flat = lambda a: a.reshape(a.shape[:-4] + (half, seq // 2, LANES))
    if s0 is None:
        s0l = jnp.zeros((HEAD, HEAD, 2 * half), F32)
    else:
        s0l = jnp.transpose(s0, (3, 4, 1, 0, 2)).reshape(HEAD, HEAD, 2 * half)
    yf, yb, off_f, off_b, st = _scan(flat(shared), flat(directed), s0l)
    s_final = jnp.transpose(st.reshape(HEAD, HEAD, 2, batch, HEADS), (3, 2, 4, 0, 1))

    x1 = _mix(yf, yb, off_f, off_b, bonus, g, yh, x, mod, prm)
    out = _ffn(x1.reshape(tokens, D_MODEL), mod, seq, on_grid, prm)
    return out.reshape(batch, seq, D_MODEL), s_final


def kernel(x_prompt, x_sample, state_rwkv, c, c_ctx, ada_w, ada_b, w_in, conv_in, filt_w1, filt_b1, filt_w2, filt_b2, filt_w3, filt_freq, hyena_bias, rwkv_w0, rwkv_w_lora, rwkv_a0, rwkv_a_lora, rwkv_g_lora, rwkv_k_k, rwkv_k_a, rwkv_r_k, lnx_g, lnx_b, w_out, ln1_g, ln1_b, w_ffn_up, ffn_conv_w, ffn_conv_b, w_ffn_down, ln2_g, ln2_b):
    dec_batch = x_sample.shape[0]
    ones = jnp.asarray(_head_ones())
    y_prompt, y_sample = x_prompt, x_sample
    ctx_states = []
    for layer in range(DEPTH):
        row = lambda a: a[layer][None]
        prm = dict(
            ones=ones, w_in=w_in[layer].astype(BF16), conv_in=conv_in[layer],
            filt_w1=filt_w1[layer], filt_b1=filt_b1[layer], filt_w2=filt_w2[layer], filt_b2=filt_b2[layer],
            filt_w3=filt_w3[layer], filt_freq=filt_freq[layer], hyena_bias=row(hyena_bias),
            rwkv_w0=rwkv_w0[layer], rwkv_w_lora=rwkv_w_lora[layer], rwkv_a0=rwkv_a0[layer],
            rwkv_a_lora=rwkv_a_lora[layer], rwkv_g_lora=rwkv_g_lora[layer], rwkv_k_k=row(rwkv_k_k),
            rwkv_k_a=row(rwkv_k_a), rwkv_r_k=rwkv_r_k[layer].reshape(1, D_RWKV),
            lnx_g=row(lnx_g), lnx_b=row(lnx_b), w_out=w_out[layer].astype(BF16),
            ln1_g=row(ln1_g), ln1_b=row(ln1_b), w_ffn_up=w_ffn_up[layer].astype(BF16),
            ffn_conv_w=ffn_conv_w[layer].reshape(9, D_FF), ffn_conv_b=row(ffn_conv_b),
            w_ffn_down=(0.5 * w_ffn_down[layer]).astype(BF16),
            ln2_g=row(ln2_g), ln2_b=row(ln2_b))
        pad = (-(dec_batch + 1)) % SUBLANES
        cond = jnp.concatenate([c, c_ctx[None], jnp.zeros((pad, D_MODEL), F32)], axis=0)
        mod = _ada(cond, ada_w[layer], row(ada_b)).reshape(cond.shape[0], 6, D_MODEL)
        y_prompt, s_ctx = _layer(y_prompt, mod[dec_batch:dec_batch + 1], None, False, prm)
        ctx_states.append(s_ctx)
        y_sample, _ = _layer(y_sample, mod[:dec_batch], state_rwkv[:, layer], True, prm)
    return (y_prompt, y_sample, jnp.stack(ctx_states, axis=1))
```

```python
import functools
import math

import ml_dtypes
import numpy as np
import jax
import jax.numpy as jnp
from jax import lax
from jax.experimental import pallas as pl
from jax.experimental.pallas import tpu as pltpu

F32 = jnp.float32
BF16 = jnp.bfloat16

D_MODEL = 1024
D_HYENA = 512
D_RWKV = 512
HEAD = 64
HEADS = D_RWKV // HEAD
LORA_W = 64
LORA_A = 64
LORA_G = 128
LORA_ALL = LORA_W + LORA_A + LORA_G
C_IN = 3 * D_HYENA + 3 * D_RWKV + LORA_ALL
FILT_BANDS = 16
FILT_FEAT = 1 + 2 * FILT_BANDS
FILT_HIDDEN = 64
N_FILT = 2 * D_HYENA
HYENA_TARGET = 1e-2
HYENA_FAST_PCT = 0.3
HYENA_SLOW_PCT = 1.5
D_FF = 2816
GRID_W = 64
DEPTH = 1
ALPHA = (2.0 * DEPTH) ** 0.25
LN_EPS = 1e-5
GN_EPS = 64e-5
NORM_EPS = 1e-12

LANES = 128
SUBLANES = 8
MXU_N = 256
ROW_TILE = 1024
SCAN_STEPS = 32
VMEM_LIMIT = 56 * 1024 * 1024


def _cparams(sem):
    return pltpu.CompilerParams(dimension_semantics=sem, vmem_limit_bytes=VMEM_LIMIT)


def _dot(a, b):
    return jnp.dot(a, b, preferred_element_type=F32)


def _split(x):
    hi = x.astype(BF16)
    lo = (x - hi.astype(F32)).astype(BF16)
    return hi, lo


def _dot_hi_const(x, c):
    hi, lo = _split(x)
    return _dot(hi, c) + _dot(lo, c)


def _dot3(ah, al, bh, bl):
    return _dot(ah, bh) + _dot(ah, bl) + _dot(al, bh)


def _layernorm_rows(x):
    mu = jnp.mean(x, axis=-1, keepdims=True)
    xc = x - mu
    var = jnp.mean(xc * xc, axis=-1, keepdims=True)
    return xc * lax.rsqrt(var + LN_EPS)


def _row_tile(tokens, seq):
    for tm in (ROW_TILE, 512, 256):
        if tokens % tm == 0 and (tm % seq == 0 or seq % tm == 0):
            return tm
    raise ValueError(f"no row tile for {tokens} tokens of sequence length {seq}")


def _np_split(x64):
    hi = x64.astype(ml_dtypes.bfloat16)
    lo = (x64 - hi.astype(np.float64)).astype(ml_dtypes.bfloat16)
    return hi, lo


@functools.lru_cache(maxsize=None)
def _dft_constants(L):
    N = 2 * L
    f = np.arange(L, dtype=np.int64)[:, None]
    t = np.arange(L, dtype=np.int64)[None, :]
    ang = (2.0 * np.pi / N) * ((f * t) % N).astype(np.float64)
    cos, sin = np.cos(ang), np.sin(ang)
    nyq = np.where(np.arange(L) % 2 == 0, 1.0, -1.0)
    fwd_s = sin.copy()
    fwd_s[0, :] = nyq
    fwd = np.concatenate([cos, fwd_s], axis=0)
    wf = np.full((L, 1), 2.0)
    wf[0, 0] = 1.0
    inv_c = (wf * cos).T / N
    inv_s = (2.0 * sin).T / N
    inv_s[:, 0] = nyq / N
    inv = np.concatenate([inv_c, inv_s], axis=1)
    return _np_split(fwd) + _np_split(inv)


@functools.lru_cache(maxsize=None)
def _filter_constants(L):
    t = np.arange(L, dtype=np.float64)[:, None] / L
    bands = np.arange(1, FILT_BANDS + 1, dtype=np.float64)[None, :]
    ang = (2.0 * math.pi) * bands * t
    feat = np.concatenate([t, np.sin(ang), np.cos(ang)], axis=-1)
    feat = np.pad(feat, ((0, 0), (0, FILT_HIDDEN - FILT_FEAT)))
    slow = abs(math.log(HYENA_TARGET) / HYENA_SLOW_PCT)
    fast = abs(math.log(HYENA_TARGET) / HYENA_FAST_PCT)
    deltas = np.linspace(slow, fast, N_FILT, dtype=np.float64)
    window = np.exp(-t * deltas[None, :])
    return feat.astype(np.float32), window.astype(np.float32)


@functools.lru_cache(maxsize=None)
def _head_ones():
    h = np.arange(D_RWKV) // HEAD
    return (h[:, None] == h[None, :]).astype(ml_dtypes.bfloat16)


def _ada_kernel(c_ref, w_ref, b_ref, o_ref):
    c = c_ref[...]
    s = c * jax.nn.sigmoid(c)
    o_ref[...] = _dot(s.astype(BF16), w_ref[...].astype(BF16)) + b_ref[...]


def _ada(cond, ada_w, ada_b):
    rows = cond.shape[0]
    tn = 1024
    return pl.pallas_call(
        _ada_kernel,
        grid=(6 * D_MODEL // tn,),
        in_specs=[pl.BlockSpec((rows, D_MODEL), lambda j: (0, 0)),
                  pl.BlockSpec((D_MODEL, tn), lambda j: (0, j)),
                  pl.BlockSpec((1, tn), lambda j: (0, j))],
        out_specs=pl.BlockSpec((rows, tn), lambda j: (0, j)),
        out_shape=jax.ShapeDtypeStruct((rows, 6 * D_MODEL), F32),
        compiler_params=_cparams(("arbitrary",)),
        name="ada_mod",
    )(cond, ada_w, ada_b)


def _inproj_kernel(x_ref, mod_ref, w_ref, cw_ref, o_ref, h_ref, mask_ref, *, seq):
    @pl.when(pl.program_id(1) == 0)
    def _():
        m = mod_ref[0]
        h = _layernorm_rows(x_ref[...]) * (1.0 + m[1:2]) + m[0:1]
        h_ref[...] = h.astype(BF16)
        t = lax.broadcasted_iota(jnp.int32, mask_ref.shape[1:], 0) & (seq - 1)
        mask_ref[0] = (t != 0).astype(F32)
        mask_ref[1] = (t != seq - 1).astype(F32)

    p = _dot(h_ref[...], w_ref[...])
    rows = p.shape[0]
    wide = lambda k: jnp.concatenate([mask_ref[k]] * (p.shape[1] // LANES), axis=1)
    prev = pltpu.roll(p, 1, 0) * wide(0)
    nxt = pltpu.roll(p, rows - 1, 0) * wide(1)
    cw = cw_ref[...]
    o_ref[...] = cw[0:1] * prev + cw[1:2] * p + cw[2:3] * nxt


def _inproj(x2, mod, w_in_bf, conv_in, seq):
    tokens = x2.shape[0]
    tm = _row_tile(tokens, seq)
    nmod = mod.shape[0]
    assert nmod == 1 or tm <= seq
    tn = MXU_N
    return pl.pallas_call(
        functools.partial(_inproj_kernel, seq=seq),
        grid=(tokens // tm, C_IN // tn),
        in_specs=[pl.BlockSpec((tm, D_MODEL), lambda i, j: (i, 0)),
                  pl.BlockSpec((1, 6, D_MODEL), lambda i, j: ((i * tm // seq) % nmod, 0, 0)),
                  pl.BlockSpec((D_MODEL, tn), lambda i, j: (0, j)),
                  pl.BlockSpec((3, tn), lambda i, j: (0, j))],
        out_specs=pl.BlockSpec((tm, tn), lambda i, j: (i, j)),
        out_shape=jax.ShapeDtypeStruct((tokens, C_IN), F32),
        scratch_shapes=[pltpu.VMEM((tm, D_MODEL), BF16), pltpu.VMEM((2, tm, LANES), F32)],
        compiler_params=_cparams(("arbitrary", "arbitrary")),
        name="inproj_conv",
    )(x2, mod, w_in_bf, conv_in)


def _filter_kernel(feat_ref, win_ref, w1_ref, b1_ref, w2_ref, b2_ref, w3_ref, fr_ref,
                   fh_ref, fl_ref, a_ref, an_ref, b_ref, *, seq):
    def hp_dot(x, w):
        xh, xl = _split(x)
        wh, wl = _split(w)
        return _dot3(xh, xl, wh, wl)

    fr = fr_ref[...]
    h = jnp.sin(fr[0:1] * (hp_dot(feat_ref[...], w1_ref[...]) + b1_ref[...]))
    h = jnp.sin(fr[1:2] * (hp_dot(h, w2_ref[...]) + b2_ref[...]))
    h = hp_dot(h, w3_ref[...]) * win_ref[...]
    col = jnp.sum(jnp.abs(h), axis=0, keepdims=True)
    inv = 1.0 / (col[:, :D_HYENA] + col[:, D_HYENA:])
    hf = h[:, :D_HYENA] * inv
    hb = h[:, D_HYENA:] * inv
    row = lax.broadcasted_iota(jnp.int32, hf.shape, 0)
    hb = jnp.where(row == 0, 0.0, hb)
    eh, el = _split(hf + hb)
    oh, ol = _split(hf - hb)
    fh, fl = fh_ref[...], fl_ref[...]
    fe = _dot3(fh, fl, eh, el)
    fo = _dot3(fh[seq:], fl[seq:], oh, ol)
    a = fe[:seq]
    a_ref[...] = a
    an_ref[...] = jnp.where(row == 0, fe[seq:seq + 1], a)
    b_ref[...] = jnp.where(row == 0, 0.0, fo)


def _hyena_filter_spectrum(seq, w1, b1, w2, b2, w3, freq):
    feat, window = _filter_constants(seq)
    fh, fl, _, _ = _dft_constants(seq)
    w1p = jnp.pad(w1, ((0, FILT_HIDDEN - FILT_FEAT), (0, 0)))
    shp = jax.ShapeDtypeStruct((seq, D_HYENA), F32)
    return pl.pallas_call(
        functools.partial(_filter_kernel, seq=seq),
        out_shape=(shp, shp, shp),
        compiler_params=pltpu.CompilerParams(vmem_limit_bytes=VMEM_LIMIT),
        name="hyena_filter",
    )(feat, window, w1p, b1[None], w2, b2[None], w3, freq, fh, fl)


def _hyena_kernel(hv_ref, hx0_ref, hx1_ref, fh_ref, fl_ref, gh_ref, gl_ref, a_ref, an_ref, b_ref,
                  bias_ref, o_ref, *, seq):
    z = hx1_ref[...] * hv_ref[...]
    zb = z.astype(BF16)
    zf = _dot(fh_ref[...], zb) + _dot(fl_ref[...], zb)
    zc, zs = zf[:seq], zf[seq:]
    a, an, b = a_ref[...], an_ref[...], b_ref[...]
    yc = zc * a - zs * b
    ys = zc * b + zs * an
    yb = jnp.concatenate([yc, ys], axis=0).astype(BF16)
    y = _dot(gh_ref[...], yb) + _dot(gl_ref[...], yb)
    o_ref[...] = (hx0_ref[...] * (y + bias_ref[...] * z)).astype(o_ref.dtype)


def _hyena(p3, spec, bias, seq):
    batch = p3.shape[0]
    fh, fl, gh, gl = _dft_constants(seq)
    a, an, b = spec
    cb = MXU_N
    ncb = D_HYENA // cb
    const = lambda shape: pl.BlockSpec(shape, lambda c, i: (0, 0), pipeline_mode=pl.Buffered(1))
    col = lambda k: pl.BlockSpec((None, seq, cb), lambda c, i: (i, 0, k * ncb + c))
    chan = lambda rows: pl.BlockSpec((rows, cb), lambda c, i: (0, c))
    return pl.pallas_call(
        functools.partial(_hyena_kernel, seq=seq),
        grid=(ncb, batch),
        in_specs=[col(0), col(1), col(2),
                  const((2 * seq, seq)), const((2 * seq, seq)), const((seq, 2 * seq)), const((seq, 2 * seq)),
                  chan(seq), chan(seq), chan(seq), chan(1)],
        out_specs=pl.BlockSpec((None, seq, cb), lambda c, i: (i, 0, c)),
        out_shape=jax.ShapeDtypeStruct((batch, seq, D_HYENA), BF16),
        compiler_params=_cparams(("arbitrary", "arbitrary")),
        name="hyena_conv",
    )(p3, p3, p3, fh, fl, gh, gl, a, an, b, bias)


PAIR_TILE = 256


def _store_paired(x, xs_ref, put):
    rows = x.shape[0]
    for hp in range(HEADS // 2):
        xs_ref[hp] = x[:, hp * LANES:(hp + 1) * LANES]
    low = lax.broadcasted_iota(jnp.int32, (rows // 2, LANES), 1) < HEAD
    for hp in range(HEADS // 2):
        even = xs_ref[hp, pl.ds(0, rows // 2, stride=2), :]
        odd = xs_ref[hp, pl.ds(1, rows // 2, stride=2), :]
        put(2 * hp, jnp.where(low, even, pltpu.roll(odd, HEAD, 1)))
        put(2 * hp + 1, jnp.where(low, pltpu.roll(even, HEAD, 1), odd))


def _load_paired(get, xs_ref):
    rows = xs_ref.shape[1]
    low = lax.broadcasted_iota(jnp.int32, (rows // 2, LANES), 1) < HEAD
    for hp in range(HEADS // 2):
        b0, b1 = get(2 * hp), get(2 * hp + 1)
        xs_ref[hp, pl.ds(0, rows // 2, stride=2), :] = jnp.where(low, b0, pltpu.roll(b1, HEAD, 1))
        xs_ref[hp, pl.ds(1, rows // 2, stride=2), :] = jnp.where(low, pltpu.roll(b0, HEAD, 1), b1)
    return jnp.concatenate([xs_ref[hp] for hp in range(HEADS // 2)], axis=1)


def _rwkv_prep_kernel(r_ref, k_ref, v_ref, lora_ref, ones_ref, w0_ref, wl_ref, a0_ref, al_ref, gl_ref,
                      kk_ref, ka_ref, rk_ref,
                      sh_ref, dr_ref, bonus_ref, g_ref, xs_ref):
    r, k, v = r_ref[...], k_ref[...], v_ref[...]
    lora = lora_ref[...]
    wd = lora[:, :LORA_W]
    ad = lora[:, LORA_W:LORA_W + LORA_A]
    gd = lora[:, LORA_W + LORA_A:]
    ones = ones_ref[...]

    def shared(q):
        def put(h, val):
            sh_ref[q, h] = val
        return put

    def directed(q, d):
        def put(h, val):
            dr_ref[q, d, h] = val
        return put

    g_ref[...] = _dot(jax.nn.sigmoid(gd).astype(BF16), gl_ref[...].astype(BF16))
    kk = k * kk_ref[...]
    norm = jnp.sqrt(_dot_hi_const(kk * kk, ones))
    kk = kk / jnp.maximum(norm, NORM_EPS)
    _store_paired(r, xs_ref, shared(0))
    _store_paired(v, xs_ref, shared(1))
    _store_paired(-kk, xs_ref, shared(2))
    tw = jnp.tanh(wd).astype(BF16)
    adb = ad.astype(BF16)
    ksum = jnp.zeros_like(k)
    for d in range(2):
        w_raw = w0_ref[d:d + 1] + _dot(tw, wl_ref[d].astype(BF16))
        _store_paired(jnp.exp(-math.exp(-0.5) * jax.nn.sigmoid(w_raw)), xs_ref, directed(0, d))
        a = jax.nn.sigmoid(a0_ref[d:d + 1] + _dot(adb, al_ref[d].astype(BF16)))
        kd = k * (1.0 + (a - 1.0) * ka_ref[...])
        _store_paired(kd, xs_ref, directed(1, d))
        _store_paired(kk * a, xs_ref, directed(2, d))
        ksum = ksum + kd
    bonus_ref[...] = _dot_hi_const(r * ksum * rk_ref[...], ones) * v


def _rwkv_prep(p3, prm):
    batch, seq, _ = p3.shape
    tm = PAIR_TILE
    cbase = 3 * D_HYENA // D_RWKV
    col = lambda k: pl.BlockSpec((None, tm, D_RWKV), lambda b, i: (b, i, cbase + k))
    full = lambda a: pl.BlockSpec(a.shape, lambda b, i: (0,) * a.ndim)
    params = [prm['ones'], prm['rwkv_w0'], prm['rwkv_w_lora'], prm['rwkv_a0'], prm['rwkv_a_lora'],
              prm['rwkv_g_lora'], prm['rwkv_k_k'], prm['rwkv_k_a'], prm['rwkv_r_k']]
    shared = jax.ShapeDtypeStruct((3, batch, HEADS, seq // 2, LANES), F32)
    directed = jax.ShapeDtypeStruct((3, 2, batch, HEADS, seq // 2, LANES), F32)
    rows = jax.ShapeDtypeStruct((batch, seq, D_RWKV), F32)
    shared_spec = pl.BlockSpec((3, None, HEADS, tm // 2, LANES), lambda b, i: (0, b, 0, i, 0))
    directed_spec = pl.BlockSpec((3, 2, None, HEADS, tm // 2, LANES), lambda b, i: (0, 0, b, 0, i, 0))
    rows_spec = pl.BlockSpec((None, tm, D_RWKV), lambda b, i: (b, i, 0))
    return pl.pallas_call(
        _rwkv_prep_kernel,
        grid=(batch, seq // tm),
        in_specs=[col(0), col(1), col(2),
                  pl.BlockSpec((None, tm, LORA_ALL), lambda b, i: (b, i, (C_IN - LORA_ALL) // LORA_ALL))]
                 + [full(a) for a in params],
        out_specs=[shared_spec, directed_spec, rows_spec, rows_spec],
        out_shape=(shared, directed, rows, rows),
        scratch_shapes=[pltpu.VMEM((HEADS // 2, tm, LANES), F32)],
        compiler_params=_cparams(("arbitrary", "arbitrary")),
        name="rwkv_prep",
    )(p3, p3, p3, p3, *params)


V_BLOCK = 8
N_SCAN_OPERANDS = 6
_R, _VV, _A, _W, _K, _B = range(N_SCAN_OPERANDS)


def _scan_kernel(*refs, mixed, m_tiles):
    if mixed:
        sf_ref, df_ref, sb_ref, db_ref, s0_ref, yf_ref, yb_ref, st_ref = refs[:8]
    else:
        sf_ref, df_ref, s0_ref, y_ref, st_ref = refs[:5]
    s_ref, in_a, in_b, yo_a, yo_b = refs[-5:]
    npair = SCAN_STEPS // 2
    tc = pl.program_id(1)
    backward = pl.program_id(0) >= m_tiles

    @pl.when(tc == 0)
    def _():
        s_ref[...] = s0_ref[...]

    low = lax.broadcasted_iota(jnp.int32, (HEAD, LANES), 1) < HEAD

    def pair_rows(ref, qq, j):
        scans = ref.shape[1]
        flat = ref.at[qq].reshape(scans * npair, LANES)
        return flat[pl.ds(j, scans, stride=npair), :]

    def stage(dst, jj):
        src_b = npair - 1 - jj
        src = jnp.where(backward, src_b, jj)
        for q in range(N_SCAN_OPERANDS):
            f_ref, b_ref = (sf_ref, sb_ref if mixed else None) if q < 3 else (df_ref, db_ref if mixed else None)
            qq = q % 3
            if mixed:
                t = jnp.concatenate([pair_rows(f_ref, qq, jj), pair_rows(b_ref, qq, src_b)], axis=0).T
                dst[0, q] = jnp.where(low, t[:HEAD], t[HEAD:])
                dst[1, q] = jnp.where(low, t[HEAD:], t[:HEAD])
            else:
                t = pair_rows(f_ref, qq, src).T
                dst[0, q] = jnp.where(backward, t[HEAD:], t[:HEAD])
                dst[1, q] = jnp.where(backward, t[:HEAD], t[HEAD:])

    sub = lax.broadcasted_iota(jnp.int32, (SUBLANES, LANES), 0)

    def block_sum(x):
        acc = x[0:SUBLANES]
        for j in range(1, HEAD // SUBLANES):
            acc = acc + x[j * SUBLANES:(j + 1) * SUBLANES]
        return acc

    def all_sum(p):
        p = p + pltpu.roll(p, 4, 0)
        p = p + pltpu.roll(p, 2, 0)
        return p + pltpu.roll(p, 1, 0)

    def fold(x, y, sh):
        m = (sub & sh) == 0
        return jnp.where(m, x, pltpu.roll(y, sh, 0)) + jnp.where(m, pltpu.roll(x, SUBLANES - sh, 0), y)

    def rows_sum(ps):
        q = [ps[j] for j in (0, 4, 2, 6, 1, 5, 3, 7)]
        s4 = [fold(q[2 * i], q[2 * i + 1], 4) for i in range(4)]
        return fold(fold(s4[0], s4[1], 2), fold(s4[2], s4[3], 2), 1)

    def compute(src, yo):
        for sidx in range(2):
            r, w, kk = src[sidx, _R], src[sidx, _W], src[sidx, _K]
            a, b = src[sidx, _A], src[sidx, _B]
            for vb in range(HEAD // V_BLOCK):
                base = vb * V_BLOCK
                sas = [all_sum(block_sum(s_ref[base + i] * a)) for i in range(V_BLOCK)]
                ps = []
                for i in range(V_BLOCK):
                    vi = base + i
                    vrow = src[sidx, _VV, vi:vi + 1, :]
                    s_ref[vi] = s_ref[vi] * w + jnp.tile(sas[i], (HEAD // SUBLANES, 1)) * b + vrow * kk
                    ps.append(block_sum(s_ref[vi] * r))
                for g in range(V_BLOCK // SUBLANES):
                    lo = sidx * HEAD + base + g * SUBLANES
                    yo[lo:lo + SUBLANES, :] = rows_sum(ps[g * SUBLANES:(g + 1) * SUBLANES])

    def emit(yo, jj):
        src_b = npair - 1 - jj
        z = yo[...].T
        if mixed:
            yf_ref[:, jj, :] = z[:HEAD]
            yb_ref[:, src_b, :] = pltpu.roll(z[HEAD:], HEAD, 1)
        else:
            y_ref[:, jnp.where(backward, src_b, jj), :] = jnp.where(backward, pltpu.roll(z, HEAD, 1), z)

    stage(in_a, 0)

    def two_pairs(i, carry):
        j0 = 2 * i
        stage(in_b, j0 + 1)
        compute(in_a, yo_a)
        stage(in_a, jnp.minimum(j0 + 2, npair - 1))
        emit(yo_a, j0)
        compute(in_b, yo_b)
        emit(yo_b, j0 + 1)
        return carry

    lax.fori_loop(0, npair // 2, two_pairs, 0)

    @pl.when(tc == pl.num_programs(1) - 1)
    def _():
        st_ref[...] = s_ref[...]


def _scan(shared, directed, s0):
    _, half, hseq, _ = shared.shape
    npair = SCAN_STEPS // 2
    n = hseq // npair
    mixed = half < LANES
    st = pl.BlockSpec((HEAD, HEAD, LANES), lambda l, t: (0, 0, l))
    stage_buf = pltpu.VMEM((2, N_SCAN_OPERANDS, HEAD, LANES), F32)
    y_buf = pltpu.VMEM((2 * HEAD, LANES), F32)
    scratch = [pltpu.VMEM((HEAD, HEAD, LANES), F32), stage_buf, stage_buf, y_buf, y_buf]
    st_shape = jax.ShapeDtypeStruct((HEAD, HEAD, 2 * half), F32)
    if mixed:
        assert 2 * half == LANES
        blk = (half, npair, LANES)
        sf = pl.BlockSpec((3,) + blk, lambda l, t: (0, 0, t, 0))
        sb = pl.BlockSpec((3,) + blk, lambda l, t: (0, 0, n - 1 - t, 0))
        df = pl.BlockSpec((3, None) + blk, lambda l, t: (0, 0, 0, t, 0))
        db = pl.BlockSpec((3, None) + blk, lambda l, t: (0, 1, 0, n - 1 - t, 0))
        yf_spec = pl.BlockSpec(blk, lambda l, t: (0, t, 0))
        yb_spec = pl.BlockSpec(blk, lambda l, t: (0, n - 1 - t, 0))
        yshape = jax.ShapeDtypeStruct((half, hseq, LANES), F32)
        yf, yb, stt = pl.pallas_call(
            functools.partial(_scan_kernel, mixed=True, m_tiles=1),
            grid=(1, n),
            in_specs=[sf, df, sb, db, st],
            out_specs=[yf_spec, yb_spec, st],
            out_shape=(yshape, yshape, st_shape),
            scratch_shapes=scratch,
            compiler_params=_cparams(("arbitrary", "arbitrary")),
            name="rwkv_scan_mixed",
        )(shared, directed, shared, directed, s0)
        return yf, yb, 0, 0, stt
    assert half % LANES == 0
    m = half // LANES
    chunk = lambda l, t: jnp.where(l >= m, n - 1 - t, t)
    blk = (LANES, npair, LANES)
    sp = pl.BlockSpec((3,) + blk, lambda l, t: (0, l % m, chunk(l, t), 0))
    dp = pl.BlockSpec((3, None) + blk, lambda l, t: (0, l // m, l % m, chunk(l, t), 0))
    yp = pl.BlockSpec((None,) + blk, lambda l, t: (l // m, l % m, chunk(l, t), 0))
    y, stt = pl.pallas_call(
        functools.partial(_scan_kernel, mixed=False, m_tiles=m),
        grid=(2 * m, n),
        in_specs=[sp, dp, st],
        out_specs=[yp, st],
        out_shape=(jax.ShapeDtypeStruct((2, half, hseq, LANES), F32), st_shape),
        scratch_shapes=scratch,
        compiler_params=_cparams(("arbitrary", "arbitrary")),
        name="rwkv_scan",
    )(shared, directed, s0)
    y2 = y.reshape(2 * half, hseq, LANES)
    return y2, y2, 0, half, stt


def _mix_kernel(yf_ref, yb_ref, bonus_ref, g_ref, yh_ref, x_ref, mod_ref, ones_ref, wo_ref, lng_ref, lnb_ref,
                l1g_ref, l1b_ref, o_ref, xs_ref):
    ones = ones_ref[...]
    y = _load_paired(lambda h: yf_ref[h] + yb_ref[h], xs_ref)
    mu = _dot_hi_const(y, ones) * (1.0 / HEAD)
    yc = y - mu
    var = _dot_hi_const(yc * yc, ones) * (1.0 / HEAD)
    yr = yc * lax.rsqrt(var + GN_EPS) * lng_ref[...] + lnb_ref[...] + bonus_ref[...]
    yr = (yr * g_ref[...]).astype(BF16)
    mix = _dot(yh_ref[...], wo_ref[:D_HYENA, :]) + _dot(yr, wo_ref[D_HYENA:, :])
    m = mod_ref[0]
    z = _layernorm_rows(ALPHA * x_ref[...] + m[2:3] * mix)
    o_ref[...] = z * l1g_ref[...] + l1b_ref[...]


def _mix(yf, yb, off_f, off_b, bonus, g, yh, x, mod, prm):
    batch, seq, _ = x.shape
    tm = PAIR_TILE
    nmod = mod.shape[0]
    yf4 = yf.reshape(-1, HEADS, seq // 2, LANES)
    yb4 = yb.reshape(-1, HEADS, seq // 2, LANES)
    bf, bb = off_f // HEADS, off_b // HEADS
    row = lambda width: pl.BlockSpec((None, tm, width), lambda b, i: (b, i, 0))
    full = lambda a: pl.BlockSpec(a.shape, lambda b, i: (0,) * a.ndim)
    params = [prm['ones'], prm['w_out'], prm['lnx_g'], prm['lnx_b'], prm['ln1_g'], prm['ln1_b']]
    return pl.pallas_call(
        _mix_kernel,
        grid=(batch, seq // tm),
        in_specs=[pl.BlockSpec((None, HEADS, tm // 2, LANES), lambda b, i: (bf + b, 0, i, 0)),
                  pl.BlockSpec((None, HEADS, tm // 2, LANES), lambda b, i: (bb + b, 0, i, 0)),
                  row(D_RWKV), row(D_RWKV), row(D_HYENA), row(D_MODEL),
                  pl.BlockSpec((1, 6, D_MODEL), lambda b, i: (b % nmod, 0, 0))]
                 + [full(a) for a in params],
        out_specs=row(D_MODEL),
        out_shape=jax.ShapeDtypeStruct((batch, seq, D_MODEL), F32),
        scratch_shapes=[pltpu.VMEM((HEADS // 2, tm, LANES), F32)],
        compiler_params=_cparams(("arbitrary", "arbitrary")),
        name="mix_outproj",
    )(yf4, yb4, bonus, g, yh, x, mod, *params)


def _ffn_kernel(x_ref, mod_ref, wu_ref, wg_ref, cw_ref, cb_ref, wd_ref, l2g_ref, l2b_ref, o_ref,
                h_ref, acc_ref, mask_ref, *, seq, on_grid):
    j = pl.program_id(1)
    width = GRID_W if on_grid else seq

    @pl.when(j == 0)
    def _():
        m = mod_ref[0]
        h = _layernorm_rows(x_ref[...]) * (1.0 + m[4:5]) + m[3:4]
        h_ref[...] = h.astype(BF16)
        acc_ref[...] = jnp.zeros_like(acc_ref)
        t = lax.broadcasted_iota(jnp.int32, mask_ref.shape[1:], 0)
        col = t & (width - 1)
        grow = (t & (seq - 1)) >> int(math.log2(GRID_W))
        mask_ref[0] = (col != 0).astype(F32)
        mask_ref[1] = (col != width - 1).astype(F32)
        mask_ref[2] = (grow != 0).astype(F32)
        mask_ref[3] = (grow != seq // GRID_W - 1).astype(F32)

    h = h_ref[...]
    mid = h.shape[0] // 2

    def rows_dot(w):
        return jnp.concatenate([_dot(h[:mid], w), _dot(h[mid:], w)], axis=0)

    u = rows_dot(wu_ref[...])
    rows = u.shape[0]
    cw = cw_ref[...]
    wide = lambda k: jnp.concatenate([mask_ref[k]] * (u.shape[1] // LANES), axis=1)
    um = pltpu.roll(u, 1, 0) * wide(0)
    up = pltpu.roll(u, rows - 1, 0) * wide(1)
    if on_grid:
        hrow = [cw[3 * di:3 * di + 1] * um + cw[3 * di + 1:3 * di + 2] * u + cw[3 * di + 2:3 * di + 3] * up
                for di in range(3)]
        if rows == seq:
            pad = jnp.zeros((GRID_W, u.shape[1]), F32)
            u = (hrow[1] + jnp.concatenate([pad, hrow[0][:rows - GRID_W]], axis=0)
                 + jnp.concatenate([hrow[2][GRID_W:], pad], axis=0))
        else:
            u = (hrow[1] + pltpu.roll(hrow[0], GRID_W, 0) * wide(2)
                 + pltpu.roll(hrow[2], rows - GRID_W, 0) * wide(3))
    else:
        u = cw[3:4] * um + cw[4:5] * u + cw[5:6] * up
    u = u + cb_ref[...]
    c0 = math.sqrt(2.0 / math.pi)
    act = u + u * jnp.tanh(u * (c0 + (c0 * 0.044715) * (u * u)))
    gt = rows_dot(wg_ref[...])
    acc_ref[...] += _dot((act * gt).astype(BF16), wd_ref[...])

    @pl.when(j == pl.num_programs(1) - 1)
    def _():
        m = mod_ref[0]
        z = _layernorm_rows(ALPHA * x_ref[...] + m[5:6] * acc_ref[...])
        o_ref[...] = z * l2g_ref[...] + l2b_ref[...]


def _ffn(x2, mod, seq, on_grid, prm):
    tokens = x2.shape[0]
    tm = _row_tile(tokens, seq)
    nmod = mod.shape[0]
    assert nmod == 1 or tm <= seq
    assert not on_grid or tm % seq == 0
    tf = MXU_N
    nf = D_FF // tf
    full = lambda a: pl.BlockSpec(a.shape, lambda i, j: (0,) * a.ndim)
    return pl.pallas_call(
        functools.partial(_ffn_kernel, seq=seq, on_grid=on_grid),
        grid=(tokens // tm, nf),
        in_specs=[pl.BlockSpec((tm, D_MODEL), lambda i, j: (i, 0)),
                  pl.BlockSpec((1, 6, D_MODEL), lambda i, j: ((i * tm // seq) % nmod, 0, 0)),
                  pl.BlockSpec((D_MODEL, tf), lambda i, j: (0, j)),
                  pl.BlockSpec((D_MODEL, tf), lambda i, j: (0, nf + j)),
                  pl.BlockSpec((9, tf), lambda i, j: (0, j)),
                  pl.BlockSpec((1, tf), lambda i, j: (0, j)),
                  pl.BlockSpec((tf, D_MODEL), lambda i, j: (j, 0)),
                  full(prm['ln2_g']), full(prm['ln2_b'])],
        out_specs=pl.BlockSpec((tm, D_MODEL), lambda i, j: (i, 0)),
        out_shape=jax.ShapeDtypeStruct((tokens, D_MODEL), F32),
        scratch_shapes=[pltpu.VMEM((tm, D_MODEL), BF16), pltpu.VMEM((tm, D_MODEL), F32),
                        pltpu.VMEM((4, tm, LANES), F32)],
        compiler_params=_cparams(("arbitrary", "arbitrary")),
        name="conv_ffn",
    )(x2, mod, prm['w_ffn_up'], prm['w_ffn_up'], prm['ffn_conv_w'], prm['ffn_conv_b'], prm['w_ffn_down'],
      prm['ln2_g'], prm['ln2_b'])


def _layer(x, mod, s0, on_grid, prm):
    batch, seq, _ = x.shape
    tokens = batch * seq
    p2 = _inproj(x.reshape(tokens, D_MODEL), mod, prm['w_in'], prm['conv_in'], seq)
    p3 = p2.reshape(batch, seq, C_IN)

    spec = _hyena_filter_spectrum(seq, prm['filt_w1'], prm['filt_b1'], prm['filt_w2'], prm['filt_b2'],
                                  prm['filt_w3'], prm['filt_freq'])
    yh = _hyena(p3, spec, prm['hyena_bias'], seq)

    shared, directed, bonus, g = _rwkv_prep(p3, prm)
    half = batch * HEADS
    flat = lambda a: a.reshape(a.shape[:-4] + (half, seq // 2, LANES))
    if s0 is None:
        s0l = jnp.zeros((HEAD, HEAD, 2 * half), F32)
    else:
        s0l = jnp.transpose(s0, (3, 4, 1, 0, 2)).reshape(HEAD, HEAD, 2 * half)
    yf, yb, off_f, off_b, st = _scan(flat(shared), flat(directed), s0l)
    s_final = jnp.transpose(st.reshape(HEAD, HEAD, 2, batch, HEADS), (3, 2, 4, 0, 1))

    x1 = _mix(yf, yb, off_f, off_b, bonus, g, yh, x, mod, prm)
    out = _ffn(x1.reshape(tokens, D_MODEL), mod, seq, on_grid, prm)
    return out.reshape(batch, seq, D_MODEL), s_final


def kernel(x_prompt, x_sample, state_rwkv, c, c_ctx, ada_w, ada_b, w_in, conv_in, filt_w1, filt_b1, filt_w2, filt_b2, filt_w3, filt_freq, hyena_bias, rwkv_w0, rwkv_w_lora, rwkv_a0, rwkv_a_lora, rwkv_g_lora, rwkv_k_k, rwkv_k_a, rwkv_r_k, lnx_g, lnx_b, w_out, ln1_g, ln1_b, w_ffn_up, ffn_conv_w, ffn_conv_b, w_ffn_down, ln2_g, ln2_b):
    dec_batch = x_sample.shape[0]
    ones = jnp.asarray(_head_ones())
    y_prompt, y_sample = x_prompt, x_sample
    ctx_states = []
    for layer in range(DEPTH):
        row = lambda a: a[layer][None]
        prm = dict(
            ones=ones, w_in=w_in[layer].astype(BF16), conv_in=conv_in[layer],
            filt_w1=filt_w1[layer], filt_b1=filt_b1[layer], filt_w2=filt_w2[layer], filt_b2=filt_b2[layer],
            filt_w3=filt_w3[layer], filt_freq=filt_freq[layer], hyena_bias=row(hyena_bias),
            rwkv_w0=rwkv_w0[layer], rwkv_w_lora=rwkv_w_lora[layer], rwkv_a0=rwkv_a0[layer],
            rwkv_a_lora=rwkv_a_lora[layer], rwkv_g_lora=rwkv_g_lora[layer], rwkv_k_k=row(rwkv_k_k),
            rwkv_k_a=row(rwkv_k_a), rwkv_r_k=rwkv_r_k[layer].reshape(1, D_RWKV),
            lnx_g=row(lnx_g), lnx_b=row(lnx_b), w_out=w_out[layer].astype(BF16),
            ln1_g=row(ln1_g), ln1_b=row(ln1_b), w_ffn_up=w_ffn_up[layer].astype(BF16),
            ffn_conv_w=ffn_conv_w[layer].reshape(9, D_FF), ffn_conv_b=row(ffn_conv_b),
            w_ffn_down=(0.5 * w_ffn_down[layer]).astype(BF16),
            ln2_g=row(ln2_g), ln2_b=row(ln2_b))
        pad = (-(dec_batch + 1)) % SUBLANES
        cond = jnp.concatenate([c, c_ctx[None], jnp.zeros((pad, D_MODEL), F32)], axis=0)
        mod = _ada(cond, ada_w[layer], row(ada_b)).reshape(cond.shape[0], 6, D_MODEL)
        y_prompt, s_ctx = _layer(y_prompt, mod[dec_batch:dec_batch + 1], None, False, prm)
        ctx_states.append(s_ctx)
        y_sample, _ = _layer(y_sample, mod[:dec_batch], state_rwkv[:, layer], True, prm)
    return (y_prompt, y_sample, jnp.stack(ctx_states, axis=1))
```
